```python
import math
import jax, jax.numpy as jnp
from jax import lax
import numpy as np

D_MODEL = 1024
BATCH = 8
SEQ = 2048
DEPTH = 1
DEC_BATCH = 32
DEC_SEQ = 64
PAST_LEN = 1024

CHUNK = 64
D_MIX = D_MODEL
D_S5 = D_MIX // 2
D_POOL = D_MIX - D_S5
S5_GROUP = 16
N_S5_GROUPS = D_S5 // S5_GROUP
S5_STATE = 64
POOL_WINDOWS = (2, 4, 8, 16)
N_POOL_GROUPS = len(POOL_WINDOWS)
POOL_GROUP = D_POOL // N_POOL_GROUPS
POOL_HIST = max(POOL_WINDOWS) - 1
D_FF = 4 * D_MODEL
D_PLE = 256
EPS = 1e-6
DT_MIN = 1e-3
DT_MAX = 1e-1

kernel_name = 'hymba_s5_pool_streaming_step'


def _rmsnorm(x, g):
    xf = x.astype(jnp.float32)
    y = xf * lax.rsqrt(jnp.mean(xf * xf, axis=-1, keepdims=True) + EPS)
    return (y * g.astype(jnp.float32)).astype(x.dtype)


def _s5_mixer(u, h0, lam_re, lam_im, log_dt, b_re, b_im, c_re, c_im, d_skip, w_glu):
    bsz, L, _ = u.shape
    f32 = jnp.float32
    lam = lax.complex(lam_re.astype(f32), lam_im.astype(f32))
    dt = jnp.exp(log_dt.astype(f32))[:, None]
    a_bar = jnp.exp(lam * dt)
    b = lax.complex(b_re.astype(f32), b_im.astype(f32))
    b_bar = ((a_bar - 1.0) / lam)[..., None] * b
    c = lax.complex(c_re.astype(f32), c_im.astype(f32))
    uf = u.astype(f32)
    ug = uf.reshape(bsz, L, N_S5_GROUPS, S5_GROUP)
    bu = jnp.einsum('gph,blgh->blgp', b_bar, ug)
    bu = bu.at[:, 0].add(a_bar[None] * h0)
    a = jnp.broadcast_to(a_bar, bu.shape)

    def combine(e1, e2):
        a1, b1 = e1
        a2, b2 = e2
        return a1 * a2, a2 * b1 + b2

    _, h = lax.associative_scan(combine, (a, bu), axis=1)
    y = jnp.real(jnp.einsum('ghp,blgp->blgh', c, h)).reshape(bsz, L, D_S5)
    y = jax.nn.gelu(y + d_skip.astype(f32) * uf)
    y = y * jax.nn.sigmoid(y @ w_glu.astype(f32))
    return y.astype(u.dtype), h[:, -1]


def _pool_mixer(u, hist, start_pos, w_pool, pool_scale):
    bsz, L, _ = u.shape
    f32 = jnp.float32
    xp = jnp.concatenate([hist.astype(u.dtype), u], axis=1)
    xf = xp.astype(f32)
    cs = jnp.concatenate([jnp.zeros((bsz, 1, D_POOL), f32), jnp.cumsum(xf, axis=1)], axis=1)
    pos = start_pos + jnp.arange(L)
    end = cs[:, POOL_HIST + 1:]
    outs = []
    for g, w in enumerate(POOL_WINDOWS):
        sl = slice(g * POOL_GROUP, (g + 1) * POOL_GROUP)
        begin = cs[:, POOL_HIST + 1 - w: POOL_HIST + 1 - w + L, sl]
        cnt = jnp.minimum(w, pos + 1).astype(f32)[None, :, None]
        pooled = (end[..., sl] - begin) / cnt - xf[:, POOL_HIST:, sl]
        outs.append(pooled @ w_pool[g].astype(f32))
    y = jnp.concatenate(outs, axis=-1) * pool_scale.astype(f32)
    return y.astype(u.dtype), xp[:, -POOL_HIST:]


def _layer(x, p, h0, hist, start_pos, g_mix_norm, w_in, lambda_re, lambda_im, log_dt, b_re, b_im,
           c_re, c_im, d_skip, w_glu, w_pool, pool_scale, g_s5_out, g_pool_out, w_out,
           g_mlp_norm, w_up, w_down, g_ple_norm, w_ple_gate, w_ple_proj):
    z = _rmsnorm(x, g_mix_norm) @ w_in
    y_s5, h_last = _s5_mixer(z[..., :D_S5], h0, lambda_re, lambda_im, log_dt, b_re, b_im,
                             c_re, c_im, d_skip, w_glu)
    y_pool, hist_new = _pool_mixer(z[..., D_S5:], hist, start_pos, w_pool, pool_scale)
    mixed = jnp.concatenate([_rmsnorm(y_s5, g_s5_out), _rmsnorm(y_pool, g_pool_out)], axis=-1)
    x = x + mixed @ w_out
    x = x + jnp.square(jax.nn.relu(_rmsnorm(x, g_mlp_norm) @ w_up)) @ w_down
    gate = jax.nn.sigmoid(_rmsnorm(x, g_ple_norm) @ w_ple_gate)
    x = x + (p @ w_ple_proj) * gate
    return x, h_last, hist_new


def setup_inputs(seed: int = 0) -> dict:
    key = jax.random.key(seed)
    ks = jax.random.split(key, 32)
    f32 = jnp.float32
    nrm = lambda k, s, sc: jax.random.normal(k, s, f32) * sc
    gain = lambda k, s: 1.0 + 0.05 * jax.random.normal(k, s, f32)
    n_idx = jnp.arange(S5_STATE, dtype=f32)
    return {
        'x_prompt': nrm(ks[0], (BATCH, SEQ, D_MODEL), 1.0),
        'x_sample': nrm(ks[1], (DEC_BATCH, DEC_SEQ, D_MODEL), 1.0),
        'state_s5_re': nrm(ks[2], (DEPTH, DEC_BATCH, N_S5_GROUPS, S5_STATE), 0.1),
        'state_s5_im': nrm(ks[3], (DEPTH, DEC_BATCH, N_S5_GROUPS, S5_STATE), 0.1),
        'state_pool': nrm(ks[4], (DEPTH, DEC_BATCH, POOL_HIST, D_POOL), 1.0),
        'p_prompt': nrm(ks[5], (DEPTH, BATCH, SEQ, D_PLE), 1.0),
        'p_sample': nrm(ks[6], (DEPTH, DEC_BATCH, DEC_SEQ, D_PLE), 1.0),
        'g_mix_norm': gain(ks[7], (DEPTH, D_MODEL)),
        'w_in': nrm(ks[8], (DEPTH, D_MODEL, D_MIX), D_MODEL ** -0.5),
        'lambda_re': -0.5 + 0.01 * jax.random.normal(ks[9], (DEPTH, N_S5_GROUPS, S5_STATE), f32),
        'lambda_im': math.pi * n_idx + 0.01 * jax.random.normal(ks[10], (DEPTH, N_S5_GROUPS, S5_STATE), f32),
        'log_dt': jax.random.uniform(ks[11], (DEPTH, N_S5_GROUPS), f32, math.log(DT_MIN), math.log(DT_MAX)),
        'b_re': nrm(ks[12], (DEPTH, N_S5_GROUPS, S5_STATE, S5_GROUP), (2 * S5_GROUP) ** -0.5),
        'b_im': nrm(ks[13], (DEPTH, N_S5_GROUPS, S5_STATE, S5_GROUP), (2 * S5_GROUP) ** -0.5),
        'c_re': nrm(ks[14], (DEPTH, N_S5_GROUPS, S5_GROUP, S5_STATE), S5_STATE ** -0.5),
        'c_im': nrm(ks[15], (DEPTH, N_S5_GROUPS, S5_GROUP, S5_STATE), S5_STATE ** -0.5),
        'd_skip': nrm(ks[16], (DEPTH, D_S5), 1.0),
        'w_glu': nrm(ks[17], (DEPTH, D_S5, D_S5), D_S5 ** -0.5),
        'w_pool': nrm(ks[18], (DEPTH, N_POOL_GROUPS, POOL_GROUP, POOL_GROUP), POOL_GROUP ** -0.5),
        'pool_scale': gain(ks[19], (DEPTH, D_POOL)),
        'g_s5_out': gain(ks[20], (DEPTH, D_S5)),
        'g_pool_out': gain(ks[21], (DEPTH, D_POOL)),
        'w_out': nrm(ks[22], (DEPTH, D_MIX, D_MODEL), D_MIX ** -0.5),
        'g_mlp_norm': gain(ks[23], (DEPTH, D_MODEL)),
        'w_up': nrm(ks[24], (DEPTH, D_MODEL, D_FF), D_MODEL ** -0.5),
        'w_down': nrm(ks[25], (DEPTH, D_FF, D_MODEL), D_FF ** -0.5),
        'g_ple_norm': gain(ks[26], (DEPTH, D_MODEL)),
        'w_ple_gate': nrm(ks[27], (DEPTH, D_MODEL, D_MODEL), D_MODEL ** -0.5),
        'w_ple_proj': nrm(ks[28], (DEPTH, D_PLE, D_MODEL), D_PLE ** -0.5),
        'g_final': gain(ks[29], (D_MODEL,)),
    }


def reference(x_prompt, x_sample, state_s5_re, state_s5_im, state_pool, p_prompt, p_sample,
              g_mix_norm, w_in, lambda_re, lambda_im, log_dt, b_re, b_im, c_re, c_im, d_skip,
              w_glu, w_pool, pool_scale, g_s5_out, g_pool_out, w_out, g_mlp_norm, w_up, w_down,
              g_ple_norm, w_ple_gate, w_ple_proj, g_final):
    f32 = jnp.float32
    bsz_p = x_prompt.shape[0]
    bsz_s = x_sample.shape[0]
    hp, hs = x_prompt, x_sample
    re_p, im_p, pool_p, re_s, im_s, pool_s = [], [], [], [], [], []
    for i in range(DEPTH):
        lp = (g_mix_norm[i], w_in[i], lambda_re[i], lambda_im[i], log_dt[i], b_re[i], b_im[i],
              c_re[i], c_im[i], d_skip[i], w_glu[i], w_pool[i], pool_scale[i], g_s5_out[i],
              g_pool_out[i], w_out[i], g_mlp_norm[i], w_up[i], w_down[i], g_ple_norm[i],
              w_ple_gate[i], w_ple_proj[i])
        h0_p = jnp.zeros((bsz_p, N_S5_GROUPS, S5_STATE), jnp.complex64)
        hist_p = jnp.zeros((bsz_p, POOL_HIST, D_POOL), hp.dtype)
        hp, hl_p, hn_p = _layer(hp, p_prompt[i], h0_p, hist_p, 0, *lp)
        h0_s = lax.complex(state_s5_re[i].astype(f32), state_s5_im[i].astype(f32))
        hs, hl_s, hn_s = _layer(hs, p_sample[i], h0_s, state_pool[i], PAST_LEN, *lp)
        re_p.append(jnp.real(hl_p)); im_p.append(jnp.imag(hl_p)); pool_p.append(hn_p)
        re_s.append(jnp.real(hl_s)); im_s.append(jnp.imag(hl_s)); pool_s.append(hn_s)
    y_prompt = _rmsnorm(hp, g_final)
    y_sample = _rmsnorm(hs, g_final)
    return (y_prompt, y_sample, jnp.stack(re_p), jnp.stack(im_p), jnp.stack(pool_p),
            jnp.stack(re_s), jnp.stack(im_s), jnp.stack(pool_s))
```

```python
import functools
import math

import jax
import jax.numpy as jnp
from jax import lax
from jax.experimental import pallas as pl
from jax.experimental.pallas import tpu as pltpu

D_MODEL = 1024
D_S5 = 512
D_POOL = 512
S5_GROUP = 16
N_S5_GROUPS = 32
S5_STATE = 64
POOL_WINDOWS = (2, 4, 8, 16)
POOL_GROUP = 128
POOL_HIST = 15
HIST_FRAMES = POOL_HIST + 1
D_FF = 4096
D_PLE = 256
EPS = 1e-6
PAST_LEN = 1024

N_S5_BLOCKS = 4
S5_BLOCK_STATES = 512
S5_COLS = 2 * N_S5_GROUPS * S5_STATE
STATE_VREG_ELEMS = 4096

MIXER_ROWS = 512
FFN_ROWS = 512
FFN_CHUNK = 1024
VMEM_LIMIT_BYTES = 56 * 1024 * 1024

f32 = jnp.float32
bf16 = jnp.bfloat16


def _rmsnorm(x, g):
    y = x * lax.rsqrt(jnp.mean(x * x, axis=-1, keepdims=True) + EPS)
    return y * g


def _dot(a, b):
    return jnp.dot(a, b, preferred_element_type=f32)


def _s5_prep_kernel(lre_ref, lim_ref, logdt_ref, bre_ref, bim_ref,
                    are_ref, aim_ref, bbre_ref, bbim_ref):
    lre = lre_ref[...]
    lim = lim_ref[...]
    dt = jnp.exp(logdt_ref[...])
    mag = jnp.exp(lre * dt)
    ang = lim * dt
    are = mag * jnp.cos(ang)
    aim = mag * jnp.sin(ang)
    are_ref[...] = are
    aim_ref[...] = aim
    nr = are - 1.0
    ni = aim
    den = lre * lre + lim * lim
    cre = ((nr * lre + ni * lim) / den)[:, None, :]
    cim = ((ni * lre - nr * lim) / den)[:, None, :]
    bre = bre_ref[...]
    bim = bim_ref[...]
    bbre_ref[...] = cre * bre - cim * bim
    bbim_ref[...] = cre * bim + cim * bre


def _s5_prep(lambda_re, lambda_im, log_dt, b_re, b_im):
    g, p, h = b_re.shape
    out = jax.ShapeDtypeStruct
    return pl.pallas_call(
        _s5_prep_kernel,
        out_shape=(out((g, p), f32), out((g, p), f32), out((g, h, p), f32), out((g, h, p), f32)),
        name="s5_prep",
    )(lambda_re, lambda_im, log_dt.reshape(g, 1),
      jnp.transpose(b_re, (0, 2, 1)), jnp.transpose(b_im, (0, 2, 1)))


def _block_diag(w):
    j, g, m, n = w.shape
    eye = jnp.eye(g, dtype=w.dtype)
    return (w[:, :, :, None, :] * eye[None, :, None, :, None]).reshape(j, g * m, g * n)


def _mixer_kernel(x_ref, h0_ref, hist_ref, gmix_ref, win_ref, are_ref, aim_ref, bblk_ref, cblk_ref,
                  dskip_ref, wglu_ref, wpool_ref, pscale_ref, gs5_ref, gpool_ref, wout_ref,
                  x1_ref, hlast_ref, histout_ref,
                  bu_scr, h_scr, xp_scr, *, batch, steps, start_pos):
    c = pl.program_id(0)
    rows = batch * steps
    hist_rows = HIST_FRAMES * batch

    @pl.when(c == 0)
    def _():
        h_scr[...] = h0_ref[...]
        xp_scr[0:hist_rows, :] = hist_ref[...]

    x = x_ref[...]
    z = _dot(_rmsnorm(x, gmix_ref[...]).astype(bf16), win_ref[...])
    zs = z[:, :D_S5]
    xp_scr[hist_rows:, :] = z[:, D_S5:]

    zsb = zs.astype(bf16)
    for j in range(N_S5_BLOCKS):
        bu_scr[:, j * 1024:(j + 1) * 1024] = _dot(zsb[:, j * 128:(j + 1) * 128], bblk_ref[j])

    width = STATE_VREG_ELEMS // batch
    for k in range(N_S5_GROUPS * S5_STATE // width):
        j, o = divmod(k * width, S5_BLOCK_STATES)
        cre = j * 2 * S5_BLOCK_STATES + o
        cim = cre + S5_BLOCK_STATES
        ar = jnp.broadcast_to(are_ref[:, k * width:(k + 1) * width], (batch, width))
        ai = jnp.broadcast_to(aim_ref[:, k * width:(k + 1) * width], (batch, width))

        def step(t, carry, cre=cre, cim=cim, ar=ar, ai=ai):
            hr, hi = carry
            r0 = pl.multiple_of(t * batch, batch)
            nr = ar * hr - ai * hi + bu_scr[pl.ds(r0, batch), cre:cre + width]
            ni = ar * hi + ai * hr + bu_scr[pl.ds(r0, batch), cim:cim + width]
            bu_scr[pl.ds(r0, batch), cre:cre + width] = nr
            bu_scr[pl.ds(r0, batch), cim:cim + width] = ni
            return nr, ni

        hr, hi = lax.fori_loop(0, steps, step,
                               (h_scr[:, cre:cre + width], h_scr[:, cim:cim + width]), unroll=8)
        h_scr[:, cre:cre + width] = hr
        h_scr[:, cim:cim + width] = hi

    ys = [_dot(bu_scr[:, j * 1024:(j + 1) * 1024].astype(bf16), cblk_ref[j])
          for j in range(N_S5_BLOCKS)]
    y = jnp.concatenate(ys, axis=-1)
    y = jax.nn.gelu(y + dskip_ref[...] * zs)
    y = y * jax.nn.sigmoid(_dot(y.astype(bf16), wglu_ref[...]))
    y_s5 = _rmsnorm(y, gs5_ref[...])

    t_idx = lax.shift_right_logical(lax.broadcasted_iota(jnp.int32, (rows, 1), 0),
                                    int(math.log2(batch)))
    pos1 = start_pos + c * steps + t_idx + 1
    outs = []
    for g, w in enumerate(POOL_WINDOWS):
        ext = xp_scr[:, g * POOL_GROUP:(g + 1) * POOL_GROUP]
        s = ext
        k = 1
        while k < w:
            s = s[k * batch:] + s[:-k * batch]
            k *= 2
        win = s[(HIST_FRAMES - (w - 1)) * batch:]
        cnt = jnp.minimum(pos1, w).astype(f32)
        pooled = win * (1.0 / cnt) - ext[hist_rows:]
        outs.append(_dot(pooled.astype(bf16), wpool_ref[g]))
    y_pool = _rmsnorm(jnp.concatenate(outs, axis=-1) * pscale_ref[...], gpool_ref[...])

    mixed = jnp.concatenate([y_s5, y_pool], axis=-1).astype(bf16)
    x1_ref[...] = x + _dot(mixed, wout_ref[...])

    xp_scr[0:hist_rows, :] = xp_scr[rows:rows + hist_rows, :]

    @pl.when(c == pl.num_programs(0) - 1)
    def _():
        hlast_ref[...] = h_scr[...]
        histout_ref[...] = xp_scr[0:hist_rows, :]


def _const_spec(shape):
    return pl.BlockSpec(shape, lambda c: (0,) * len(shape), pipeline_mode=pl.Buffered(1))


def _mixer(x_tb, h0, hist, batch, start_pos, wts):
    n_rows = x_tb.shape[0]
    assert MIXER_ROWS % batch == 0 and n_rows % MIXER_ROWS == 0
    steps = MIXER_ROWS // batch
    assert steps >= HIST_FRAMES and STATE_VREG_ELEMS % batch == 0
    hist_rows = HIST_FRAMES * batch
    kern = functools.partial(_mixer_kernel, batch=batch, steps=steps, start_pos=start_pos)
    out = jax.ShapeDtypeStruct
    return pl.pallas_call(
        kern,
        grid=(n_rows // MIXER_ROWS,),
        in_specs=[pl.BlockSpec((MIXER_ROWS, D_MODEL), lambda c: (c, 0)),
                  _const_spec(h0.shape), _const_spec(hist.shape)] + [_const_spec(w.shape) for w in wts],
        out_specs=(pl.BlockSpec((MIXER_ROWS, D_MODEL), lambda c: (c, 0)),
                   pl.BlockSpec(h0.shape, lambda c: (0, 0)),
                   pl.BlockSpec(hist.shape, lambda c: (0, 0))),
        out_shape=(out((n_rows, D_MODEL), f32), out(h0.shape, f32), out(hist.shape, f32)),
        scratch_shapes=[pltpu.VMEM((MIXER_ROWS, S5_COLS), f32),
                        pltpu.VMEM((batch, S5_COLS), f32),
                        pltpu.VMEM((hist_rows + MIXER_ROWS, D_POOL), f32)],
        compiler_params=pltpu.CompilerParams(dimension_semantics=("arbitrary",),
                                             vmem_limit_bytes=VMEM_LIMIT_BYTES),
        name="mixer",
    )(x_tb, h0, hist, *wts)


def _ffn_kernel(x_ref, p_ref, gmlp_ref, wup_ref, wdown_ref, gple_ref, wgate_ref, wproj_ref,
                gfinal_ref, o_ref):
    x = x_ref[...]
    xn = _rmsnorm(x, gmlp_ref[...]).astype(bf16)
    acc = x
    for f in range(D_FF // FFN_CHUNK):
        up = _dot(xn, wup_ref[:, f * FFN_CHUNK:(f + 1) * FFN_CHUNK])
        act = jnp.square(jnp.maximum(up, 0.0)).astype(bf16)
        acc = acc + _dot(act, wdown_ref[f * FFN_CHUNK:(f + 1) * FFN_CHUNK, :])
    gate = jax.nn.sigmoid(_dot(_rmsnorm(acc, gple_ref[...]).astype(bf16), wgate_ref[...]))
    x3 = acc + _dot(p_ref[...].astype(bf16), wproj_ref[...]) * gate
    o_ref[...] = _rmsnorm(x3, gfinal_ref[...])


def _ffn(x1, p, wts):
    n_rows = x1.shape[0]
    assert n_rows % FFN_ROWS == 0
    gmlp, wup, wdown, gple, wgate, wproj, gfinal = wts
    return pl.pallas_call(
        _ffn_kernel,
        grid=(n_rows // FFN_ROWS,),
        in_specs=[pl.BlockSpec((FFN_ROWS, D_MODEL), lambda i: (i, 0)),
                  pl.BlockSpec((FFN_ROWS, D_PLE), lambda i: (i, 0))] + [_const_spec(w.shape) for w in wts],
        out_specs=pl.BlockSpec((FFN_ROWS, D_MODEL), lambda i: (i, 0)),
        out_shape=jax.ShapeDtypeStruct((n_rows, D_MODEL), f32),
        compiler_params=pltpu.CompilerParams(dimension_semantics=("arbitrary",),
                                             vmem_limit_bytes=VMEM_LIMIT_BYTES),
        name="ffn",
    )(x1, p, *wts)


def _to_time_major(a):
    b, l, d = a.shape
    return jnp.transpose(a, (1, 0, 2)).reshape(l * b, d)


def _from_time_major(a, b):
    n, d = a.shape
    return jnp.transpose(a.reshape(n // b, b, d), (1, 0, 2))


def _pack_state(re, im):
    b = re.shape[0]
    return jnp.concatenate([re.reshape(b, N_S5_BLOCKS, S5_BLOCK_STATES),
                            im.reshape(b, N_S5_BLOCKS, S5_BLOCK_STATES)], axis=2).reshape(b, S5_COLS)


def _unpack_state(h):
    b = h.shape[0]
    h = h.reshape(b, N_S5_BLOCKS, 2, S5_BLOCK_STATES)
    return (h[:, :, 0].reshape(b, N_S5_GROUPS, S5_STATE), h[:, :, 1].reshape(b, N_S5_GROUPS, S5_STATE))


def _pack_hist(hist):
    b = hist.shape[0]
    h = jnp.pad(jnp.transpose(hist, (1, 0, 2)), ((1, 0), (0, 0), (0, 0)))
    return h.reshape(HIST_FRAMES * b, D_POOL)


def _unpack_hist(h, b):
    return jnp.transpose(h.reshape(HIST_FRAMES, b, D_POOL)[1:], (1, 0, 2))


def kernel(x_prompt, x_sample, state_s5_re, state_s5_im, state_pool, p_prompt, p_sample, g_mix_norm, w_in, lambda_re, lambda_im, log_dt, b_re, b_im, c_re, c_im, d_skip, w_glu, w_pool, pool_scale, g_s5_out, g_pool_out, w_out, g_mlp_norm, w_up, w_down, g_ple_norm, w_ple_gate, w_ple_proj, g_final):
    assert x_prompt.shape[2] == D_MODEL and w_in.shape[0] == 1
    bsz_p, bsz_s = x_prompt.shape[0], x_sample.shape[0]
    row = lambda v: v.reshape(1, -1)

    a_re, a_im, bb_re, bb_im = _s5_prep(lambda_re[0], lambda_im[0], log_dt[0], b_re[0], b_im[0])
    blk = lambda w: w.reshape(N_S5_BLOCKS, 8, *w.shape[1:])
    bblk = jnp.concatenate([_block_diag(blk(bb_re)), _block_diag(blk(bb_im))], axis=2).astype(bf16)
    c_t = lambda w: blk(jnp.transpose(w, (0, 2, 1)))
    cblk = jnp.concatenate([_block_diag(c_t(c_re[0])), _block_diag(c_t(-c_im[0]))], axis=1).astype(bf16)

    mixer_wts = (row(g_mix_norm[0]), w_in[0].astype(bf16), row(a_re), row(a_im), bblk, cblk,
                 row(d_skip[0]), w_glu[0].astype(bf16), w_pool[0].astype(bf16), row(pool_scale[0]),
                 row(g_s5_out[0]), row(g_pool_out[0]), w_out[0].astype(bf16))
    ffn_wts = (row(g_mlp_norm[0]), w_up[0].astype(bf16), w_down[0].astype(bf16), row(g_ple_norm[0]),
               w_ple_gate[0].astype(bf16), w_ple_proj[0].astype(bf16), row(g_final))

    def run(x, p, h0_re, h0_im, hist, start_pos):
        b = x.shape[0]
        x1, h_last, hist_new = _mixer(_to_time_major(x), _pack_state(h0_re, h0_im), _pack_hist(hist),
                                      b, start_pos, mixer_wts)
        y = _ffn(x1, _to_time_major(p), ffn_wts)
        re, im = _unpack_state(h_last)
        return _from_time_major(y, b), re[None], im[None], _unpack_hist(hist_new, b)[None]

    zeros_state = jnp.zeros((bsz_p, N_S5_GROUPS, S5_STATE), f32)
    y_p, re_p, im_p, pool_p = run(x_prompt, p_prompt[0], zeros_state, zeros_state,
                                  jnp.zeros((bsz_p, POOL_HIST, D_POOL), f32), 0)
    y_s, re_s, im_s, pool_s = run(x_sample, p_sample[0], state_s5_re[0], state_s5_im[0],
                                  state_pool[0], PAST_LEN)
    return (y_p, y_s, re_p, im_p, pool_p, re_s, im_s, pool_s)
```

```python
import functools
import math

import jax
import jax.numpy as jnp
from jax import lax
from jax.experimental import pallas as pl
from jax.experimental.pallas import tpu as pltpu

D_MODEL = 1024
D_S5 = 512
D_POOL = 512
S5_GROUP = 16
N_S5_GROUPS = 32
S5_STATE = 64
POOL_WINDOWS = (2, 4, 8, 16)
POOL_GROUP = 128
POOL_HIST = 15
HIST_FRAMES = POOL_HIST + 1
D_FF = 4096
D_PLE = 256
EPS = 1e-6
PAST_LEN = 1024

N_S5_BLOCKS = 4
S5_BLOCK_STATES = 512
S5_COLS = 2 * N_S5_GROUPS * S5_STATE
SUBLANES, LANES = 8, 128
STATE_VREG_ELEMS = 4 * SUBLANES * LANES

MIXER_ROWS = 512
FFN_ROWS = 512
FFN_CHUNK = 1024
VMEM_LIMIT_BYTES = 56 * 1024 * 1024

f32 = jnp.float32
bf16 = jnp.bfloat16


def _rmsnorm(x, g):
    y = x * lax.rsqrt(jnp.mean(x * x, axis=-1, keepdims=True) + EPS)
    return y * g


def _dot(a, b):
    return jnp.dot(a, b, preferred_element_type=f32)


def _s5_prep_kernel(lre_ref, lim_ref, logdt_ref, bre_ref, bim_ref,
                    are_ref, aim_ref, bbre_ref, bbim_ref):
    lre = lre_ref[...]
    lim = lim_ref[...]
    dt = jnp.exp(logdt_ref[...])
    mag = jnp.exp(lre * dt)
    ang = lim * dt
    are = mag * jnp.cos(ang)
    aim = mag * jnp.sin(ang)
    are_ref[...] = are
    aim_ref[...] = aim
    nr = are - 1.0
    ni = aim
    den = lre * lre + lim * lim
    cre = ((nr * lre + ni * lim) / den)[:, None, :]
    cim = ((ni * lre - nr * lim) / den)[:, None, :]
    bre = bre_ref[...]
    bim = bim_ref[...]
    bbre_ref[...] = cre * bre - cim * bim
    bbim_ref[...] = cre * bim + cim * bre


def _s5_prep(lambda_re, lambda_im, log_dt, b_re, b_im):
    g, p, h = b_re.shape
    out = jax.ShapeDtypeStruct
    return pl.pallas_call(
        _s5_prep_kernel,
        out_shape=(out((g, p), f32), out((g, p), f32), out((g, h, p), f32), out((g, h, p), f32)),
        name="s5_prep",
    )(lambda_re, lambda_im, log_dt.reshape(g, 1),
      jnp.transpose(b_re, (0, 2, 1)), jnp.transpose(b_im, (0, 2, 1)))


def _block_diag(w):
    j, g, m, n = w.shape
    eye = jnp.eye(g, dtype=w.dtype)
    return (w[:, :, :, None, :] * eye[None, :, None, :, None]).reshape(j, g * m, g * n)


def _mixer_kernel(x_ref, h0_ref, hist_ref, gmix_ref, win_ref, are_ref, aim_ref, bblk_ref, cblk_ref,
                  dskip_ref, wglu_ref, wpool_ref, pscale_ref, gs5_ref, gpool_ref, wout_ref,
                  x1_ref, hlast_ref, histout_ref,
                  bu_scr, h_scr, xp_scr, zs_scr, perm_scr, *, batch, steps, start_pos):
    c = pl.program_id(0)
    rows = batch * steps
    hist_rows = HIST_FRAMES * batch
    pitch = _perm_pitch(steps)
    n_slabs = D_MODEL // LANES

    @pl.when(c == 0)
    def _():
        h_scr[...] = h0_ref[...]
        xp_scr[0:hist_rows, :] = hist_ref[...]

    x = x_ref[...].reshape(rows, D_MODEL)
    z = _dot(_rmsnorm(x, gmix_ref[...]).astype(bf16), win_ref[...])

    for k in range(n_slabs):
        for b in range(batch):
            perm_scr[k, b * pitch:b * pitch + steps, :] = z[b * steps:(b + 1) * steps,
                                                            k * LANES:(k + 1) * LANES]
    for t in range(steps):
        for k in range(n_slabs):
            v = perm_scr[k, pl.ds(t, batch, stride=pitch), :]
            if k * LANES < D_S5:
                zs_scr[t * batch:(t + 1) * batch, k * LANES:(k + 1) * LANES] = v
            else:
                xp_scr[hist_rows + t * batch:hist_rows + (t + 1) * batch,
                       k * LANES - D_S5:(k + 1) * LANES - D_S5] = v
    zs = zs_scr[...]

    zsb = zs.astype(bf16)
    for j in range(N_S5_BLOCKS):
        bu_scr[:, j * 1024:(j + 1) * 1024] = _dot(zsb[:, j * 128:(j + 1) * 128], bblk_ref[j])

    width = STATE_VREG_ELEMS // batch
    for k in range(N_S5_GROUPS * S5_STATE // width):
        j, o = divmod(k * width, S5_BLOCK_STATES)
        cre = j * 2 * S5_BLOCK_STATES + o
        cim = cre + S5_BLOCK_STATES
        ar = jnp.broadcast_to(are_ref[:, k * width:(k + 1) * width], (batch, width))
        ai = jnp.broadcast_to(aim_ref[:, k * width:(k + 1) * width], (batch, width))

        def step(t, carry, cre=cre, cim=cim, ar=ar, ai=ai):
            hr, hi = carry
            r0 = pl.multiple_of(t * batch, batch)
            nr = ar * hr - ai * hi + bu_scr[pl.ds(r0, batch), cre:cre + width]
            ni = ar * hi + ai * hr + bu_scr[pl.ds(r0, batch), cim:cim + width]
            bu_scr[pl.ds(r0, batch), cre:cre + width] = nr
            bu_scr[pl.ds(r0, batch), cim:cim + width] = ni
            return nr, ni

        hr, hi = lax.fori_loop(0, steps, step,
                               (h_scr[:, cre:cre + width], h_scr[:, cim:cim + width]), unroll=8)
        h_scr[:, cre:cre + width] = hr
        h_scr[:, cim:cim + width] = hi

    ys = [_dot(bu_scr[:, j * 1024:(j + 1) * 1024].astype(bf16), cblk_ref[j])
          for j in range(N_S5_BLOCKS)]
    y = jnp.concatenate(ys, axis=-1)
    y = jax.nn.gelu(y + dskip_ref[...] * zs)
    y = y * jax.nn.sigmoid(_dot(y.astype(bf16), wglu_ref[...]))
    y_s5 = _rmsnorm(y, gs5_ref[...])

    t_idx = lax.shift_right_logical(lax.broadcasted_iota(jnp.int32, (rows, 1), 0),
                                    int(math.log2(batch)))
    pos1 = start_pos + c * steps + t_idx + 1
    outs = []
    for g, w in enumerate(POOL_WINDOWS):
        ext = xp_scr[:, g * POOL_GROUP:(g + 1) * POOL_GROUP]
        s = ext
        k = 1
        while k < w:
            s = s[k * batch:] + s[:-k * batch]
            k *= 2
        win = s[(HIST_FRAMES - (w - 1)) * batch:]
        cnt = jnp.minimum(pos1, w).astype(f32)
        pooled = win * (1.0 / cnt) - ext[hist_rows:]
        outs.append(_dot(pooled.astype(bf16), wpool_ref[g]))
    y_pool = _rmsnorm(jnp.concatenate(outs, axis=-1) * pscale_ref[...], gpool_ref[...])

    mixed = jnp.concatenate([y_s5, y_pool], axis=-1)
    for t in range(steps):
        for k in range(n_slabs):
            perm_scr[k, pl.ds(t, batch, stride=pitch), :] = mixed[t * batch:(t + 1) * batch,
                                                                  k * LANES:(k + 1) * LANES]
    mixed = jnp.concatenate(
        [jnp.concatenate([perm_scr[k, b * pitch:b * pitch + steps, :] for k in range(n_slabs)], axis=-1)
         for b in range(batch)], axis=0).astype(bf16)
    x1_ref[...] = (x + _dot(mixed, wout_ref[...])).reshape(batch, steps, D_MODEL)

    xp_scr[0:hist_rows, :] = xp_scr[rows:rows + hist_rows, :]

    @pl.when(c == pl.num_programs(0) - 1)
    def _():
        hlast_ref[...] = h_scr[...]
        histout_ref[...] = xp_scr[0:hist_rows, :]


def _const_spec(shape):
    return pl.BlockSpec(shape, lambda c: (0,) * len(shape), pipeline_mode=pl.Buffered(1))


def _perm_pitch(steps):
    tiles = steps // SUBLANES
    return SUBLANES * (tiles + 1 if tiles % 2 == 0 else tiles + 2)


def _mixer(x, h0, hist, start_pos, wts):
    batch, length, _ = x.shape
    assert MIXER_ROWS % batch == 0
    steps = MIXER_ROWS // batch
    assert length % steps == 0 and steps % SUBLANES == 0
    assert steps >= HIST_FRAMES and STATE_VREG_ELEMS % batch == 0
    hist_rows = HIST_FRAMES * batch
    kern = functools.partial(_mixer_kernel, batch=batch, steps=steps, start_pos=start_pos)
    out = jax.ShapeDtypeStruct
    x_spec = pl.BlockSpec((batch, steps, D_MODEL), lambda c: (0, c, 0))
    return pl.pallas_call(
        kern,
        grid=(length // steps,),
        in_specs=[x_spec, _const_spec(h0.shape), _const_spec(hist.shape)]
        + [_const_spec(w.shape) for w in wts],
        out_specs=(x_spec,
                   pl.BlockSpec(h0.shape, lambda c: (0, 0)),
                   pl.BlockSpec(hist.shape, lambda c: (0, 0))),
        out_shape=(out(x.shape, f32), out(h0.shape, f32), out(hist.shape, f32)),
        scratch_shapes=[pltpu.VMEM((MIXER_ROWS, S5_COLS), f32),
                        pltpu.VMEM((batch, S5_COLS), f32),
                        pltpu.VMEM((hist_rows + MIXER_ROWS, D_POOL), f32),
                        pltpu.VMEM((MIXER_ROWS, D_S5), f32),
                        pltpu.VMEM((D_MODEL // LANES, batch * _perm_pitch(steps), LANES), f32)],
        compiler_params=pltpu.CompilerParams(dimension_semantics=("arbitrary",),
                                             vmem_limit_bytes=VMEM_LIMIT_BYTES),
        name="mixer",
    )(x, h0, hist, *wts)


def _ffn_kernel(x_ref, p_ref, gmlp_ref, wup_ref, wdown_ref, gple_ref, wgate_ref, wproj_ref,
                gfinal_ref, o_ref):
    x = x_ref[...]
    xn = _rmsnorm(x, gmlp_ref[...]).astype(bf16)
    acc = x
    for f in range(D_FF // FFN_CHUNK):
        up = _dot(xn, wup_ref[:, f * FFN_CHUNK:(f + 1) * FFN_CHUNK])
        act = jnp.square(jnp.maximum(up, 0.0)).astype(bf16)
        acc = acc + _dot(act, wdown_ref[f * FFN_CHUNK:(f + 1) * FFN_CHUNK, :])
    gate = jax.nn.sigmoid(_dot(_rmsnorm(acc, gple_ref[...]).astype(bf16), wgate_ref[...]))
    x3 = acc + _dot(p_ref[...].astype(bf16), wproj_ref[...]) * gate
    o_ref[...] = _rmsnorm(x3, gfinal_ref[...])


def _ffn(x1, p, wts):
    n_rows = x1.shape[0]
    assert n_rows % FFN_ROWS == 0
    gmlp, wup, wdown, gple, wgate, wproj, gfinal = wts
    return pl.pallas_call(
        _ffn_kernel,
        grid=(n_rows // FFN_ROWS,),
        in_specs=[pl.BlockSpec((FFN_ROWS, D_MODEL), lambda i: (i, 0)),
                  pl.BlockSpec((FFN_ROWS, D_PLE), lambda i: (i, 0))] + [_const_spec(w.shape) for w in wts],
        out_specs=pl.BlockSpec((FFN_ROWS, D_MODEL), lambda i: (i, 0)),
        out_shape=jax.ShapeDtypeStruct((n_rows, D_MODEL), f32),
        compiler_params=pltpu.CompilerParams(dimension_semantics=("arbitrary",),
                                             vmem_limit_bytes=VMEM_LIMIT_BYTES),
        name="ffn",
    )(x1, p, *wts)


def _pack_state(re, im):
    b = re.shape[0]
    return jnp.concatenate([re.reshape(b, N_S5_BLOCKS, S5_BLOCK_STATES),
                            im.reshape(b, N_S5_BLOCKS, S5_BLOCK_STATES)], axis=2).reshape(b, S5_COLS)


def _unpack_state(h):
    b = h.shape[0]
    h = h.reshape(b, N_S5_BLOCKS, 2, S5_BLOCK_STATES)
    return (h[:, :, 0].reshape(b, N_S5_GROUPS, S5_STATE), h[:, :, 1].reshape(b, N_S5_GROUPS, S5_STATE))


def _pack_hist(hist):
    b = hist.shape[0]
    h = jnp.pad(jnp.transpose(hist, (1, 0, 2)), ((1, 0), (0, 0), (0, 0)))
    return h.reshape(HIST_FRAMES * b, D_POOL)


def _unpack_hist(h, b):
    return jnp.transpose(h.reshape(HIST_FRAMES, b, D_POOL)[1:], (1, 0, 2))


def kernel(x_prompt, x_sample, state_s5_re, state_s5_im, state_pool, p_prompt, p_sample, g_mix_norm, w_in, lambda_re, lambda_im, log_dt, b_re, b_im, c_re, c_im, d_skip, w_glu, w_pool, pool_scale, g_s5_out, g_pool_out, w_out, g_mlp_norm, w_up, w_down, g_ple_norm, w_ple_gate, w_ple_proj, g_final):
    assert x_prompt.shape[2] == D_MODEL and w_in.shape[0] == 1
    bsz_p, bsz_s = x_prompt.shape[0], x_sample.shape[0]
    row = lambda v: v.reshape(1, -1)

    a_re, a_im, bb_re, bb_im = _s5_prep(lambda_re[0], lambda_im[0], log_dt[0], b_re[0], b_im[0])
    blk = lambda w: w.reshape(N_S5_BLOCKS, 8, *w.shape[1:])
    bblk = jnp.concatenate([_block_diag(blk(bb_re)), _block_diag(blk(bb_im))], axis=2).astype(bf16)
    c_t = lambda w: blk(jnp.transpose(w, (0, 2, 1)))
    cblk = jnp.concatenate([_block_diag(c_t(c_re[0])), _block_diag(c_t(-c_im[0]))], axis=1).astype(bf16)

    mixer_wts = (row(g_mix_norm[0]), w_in[0].astype(bf16), row(a_re), row(a_im), bblk, cblk,
                 row(d_skip[0]), w_glu[0].astype(bf16), w_pool[0].astype(bf16), row(pool_scale[0]),
                 row(g_s5_out[0]), row(g_pool_out[0]), w_out[0].astype(bf16))
    ffn_wts = (row(g_mlp_norm[0]), w_up[0].astype(bf16), w_down[0].astype(bf16), row(g_ple_norm[0]),
               w_ple_gate[0].astype(bf16), w_ple_proj[0].astype(bf16), row(g_final))

    def run(x, p, h0_re, h0_im, hist, start_pos):
        b, length, _ = x.shape
        x1, h_last, hist_new = _mixer(x, _pack_state(h0_re, h0_im), _pack_hist(hist), start_pos, mixer_wts)
        y = _ffn(x1.reshape(b * length, D_MODEL), p.reshape(b * length, D_PLE), ffn_wts)
        re, im = _unpack_state(h_last)
        return y.reshape(b, length, D_MODEL), re[None], im[None], _unpack_hist(hist_new, b)[None]

    zeros_state = jnp.zeros((bsz_p, N_S5_GROUPS, S5_STATE), f32)
    y_p, re_p, im_p, pool_p = run(x_prompt, p_prompt[0], zeros_state, zeros_state,
                                  jnp.zeros((bsz_p, POOL_HIST, D_POOL), f32), 0)
    y_s, re_s, im_s, pool_s = run(x_sample, p_sample[0], state_s5_re[0], state_s5_im[0],
                                  state_pool[0], PAST_LEN)
    return (y_p, y_s, re_p, im_p, pool_p, re_s, im_s, pool_s)
```

```python
import functools
import math

import jax
import jax.numpy as jnp
from jax import lax
from jax.experimental import pallas as pl
from jax.experimental.pallas import tpu as pltpu

D_MODEL = 1024
D_S5 = 512
D_POOL = 512
S5_GROUP = 16
N_S5_GROUPS = 32
S5_STATE = 64
POOL_WINDOWS = (2, 4, 8, 16)
POOL_GROUP = 128
POOL_HIST = 15
HIST_FRAMES = POOL_HIST + 1
D_FF = 4096
D_PLE = 256
EPS = 1e-6
PAST_LEN = 1024

N_S5_BLOCKS = 4
S5_BLOCK_STATES = 512
S5_COLS = 2 * N_S5_GROUPS * S5_STATE
SUBLANES, LANES = 8, 128
STATE_VREG_ELEMS = 4 * SUBLANES * LANES

MIXER_ROWS = 512
FFN_ROWS = 512
FFN_CHUNK = 1024
VMEM_LIMIT_BYTES = 56 * 1024 * 1024

f32 = jnp.float32
bf16 = jnp.bfloat16


def _rmsnorm(x, g):
    y = x * lax.rsqrt(jnp.mean(x * x, axis=-1, keepdims=True) + EPS)
    return y * g


def _dot(a, b):
    return jnp.dot(a, b, preferred_element_type=f32)


def _s5_prep_kernel(lre_ref, lim_ref, logdt_ref, bre_ref, bim_ref,
                    are_ref, aim_ref, bbre_ref, bbim_ref):
    lre = lre_ref[...]
    lim = lim_ref[...]
    dt = jnp.exp(logdt_ref[...])
    mag = jnp.exp(lre * dt)
    ang = lim * dt
    are = mag * jnp.cos(ang)
    aim = mag * jnp.sin(ang)
    are_ref[...] = are
    aim_ref[...] = aim
    nr = are - 1.0
    ni = aim
    den = lre * lre + lim * lim
    cre = ((nr * lre + ni * lim) / den)[:, None, :]
    cim = ((ni * lre - nr * lim) / den)[:, None, :]
    bre = bre_ref[...]
    bim = bim_ref[...]
    bbre_ref[...] = cre * bre - cim * bim
    bbim_ref[...] = cre * bim + cim * bre


def _s5_prep(lambda_re, lambda_im, log_dt, b_re, b_im):
    g, p, h = b_re.shape
    out = jax.ShapeDtypeStruct
    return pl.pallas_call(
        _s5_prep_kernel,
        out_shape=(out((g, p), f32), out((g, p), f32), out((g, h, p), f32), out((g, h, p), f32)),
        name="s5_prep",
    )(lambda_re, lambda_im, log_dt.reshape(g, 1),
      jnp.transpose(b_re, (0, 2, 1)), jnp.transpose(b_im, (0, 2, 1)))


def _block_diag(w):
    j, g, m, n = w.shape
    eye = jnp.eye(g, dtype=w.dtype)
    return (w[:, :, :, None, :] * eye[None, :, None, :, None]).reshape(j, g * m, g * n)


def _mixer_kernel(x_ref, h0_ref, hist_ref, gmix_ref, win_ref, are_ref, aim_ref, bblk_ref, cblk_ref,
                  dskip_ref, wglu_ref, wpool_ref, pscale_ref, gs5_ref, gpool_ref, wout_ref,
                  x1_ref, hlast_ref, histout_ref,
                  bu_scr, h_scr, xp_scr, zs_scr, perm_scr, *, batch, steps, start_pos):
    c = pl.program_id(0)
    rows = batch * steps
    hist_rows = HIST_FRAMES * batch
    pitch = _perm_pitch(steps)
    n_slabs = D_MODEL // LANES

    @pl.when(c == 0)
    def _():
        h_scr[...] = h0_ref[...]
        xp_scr[0:hist_rows, :] = hist_ref[...]

    x = x_ref[...].reshape(rows, D_MODEL)
    z = _dot(_rmsnorm(x, gmix_ref[...]).astype(bf16), win_ref[...])

    for k in range(n_slabs):
        for b in range(batch):
            perm_scr[k, b * pitch:b * pitch + steps, :] = z[b * steps:(b + 1) * steps,
                                                            k * LANES:(k + 1) * LANES]
    for t in range(steps):
        for k in range(n_slabs):
            v = perm_scr[k, pl.ds(t, batch, stride=pitch), :]
            if k * LANES < D_S5:
                zs_scr[t * batch:(t + 1) * batch, k * LANES:(k + 1) * LANES] = v
            else:
                xp_scr[hist_rows + t * batch:hist_rows + (t + 1) * batch,
                       k * LANES - D_S5:(k + 1) * LANES - D_S5] = v
    zs = zs_scr[...]

    zsb = zs.astype(bf16)
    for j in range(N_S5_BLOCKS):
        bu_scr[:, j * 1024:(j + 1) * 1024] = _dot(zsb[:, j * 128:(j + 1) * 128], bblk_ref[j])

    width = STATE_VREG_ELEMS // batch
    for k in range(N_S5_GROUPS * S5_STATE // width):
        j, o = divmod(k * width, S5_BLOCK_STATES)
        cre = j * 2 * S5_BLOCK_STATES + o
        cim = cre + S5_BLOCK_STATES
        ar = jnp.broadcast_to(are_ref[:, k * width:(k + 1) * width], (batch, width))
        ai = jnp.broadcast_to(aim_ref[:, k * width:(k + 1) * width], (batch, width))

        hr, hi = h_scr[:, cre:cre + width], h_scr[:, cim:cim + width]
        for t in range(steps):
            r0 = t * batch
            nr = ar * hr - ai * hi + bu_scr[r0:r0 + batch, cre:cre + width]
            ni = ar * hi + ai * hr + bu_scr[r0:r0 + batch, cim:cim + width]
            bu_scr[r0:r0 + batch, cre:cre + width] = nr
            bu_scr[r0:r0 + batch, cim:cim + width] = ni
            hr, hi = nr, ni
        h_scr[:, cre:cre + width] = hr
        h_scr[:, cim:cim + width] = hi

    ys = [_dot(bu_scr[:, j * 1024:(j + 1) * 1024].astype(bf16), cblk_ref[j])
          for j in range(N_S5_BLOCKS)]
    y = jnp.concatenate(ys, axis=-1)
    y = jax.nn.gelu(y + dskip_ref[...] * zs)
    y = y * jax.nn.sigmoid(_dot(y.astype(bf16), wglu_ref[...]))
    y_s5 = _rmsnorm(y, gs5_ref[...])

    t_idx = lax.shift_right_logical(lax.broadcasted_iota(jnp.int32, (rows, 1), 0),
                                    int(math.log2(batch)))
    pos1 = start_pos + c * steps + t_idx + 1
    outs = []
    for g, w in enumerate(POOL_WINDOWS):
        ext = xp_scr[:, g * POOL_GROUP:(g + 1) * POOL_GROUP]
        s = ext
        k = 1
        while k < w:
            s = s[k * batch:] + s[:-k * batch]
            k *= 2
        win = s[(HIST_FRAMES - (w - 1)) * batch:]
        cnt = jnp.minimum(pos1, w).astype(f32)
        pooled = win * (1.0 / cnt) - ext[hist_rows:]
        outs.append(_dot(pooled.astype(bf16), wpool_ref[g]))
    y_pool = _rmsnorm(jnp.concatenate(outs, axis=-1) * pscale_ref[...], gpool_ref[...])

    mixed = jnp.concatenate([y_s5, y_pool], axis=-1)
    for t in range(steps):
        for k in range(n_slabs):
            perm_scr[k, pl.ds(t, batch, stride=pitch), :] = mixed[t * batch:(t + 1) * batch,
                                                                  k * LANES:(k + 1) * LANES]
    mixed = jnp.concatenate(
        [jnp.concatenate([perm_scr[k, b * pitch:b * pitch + steps, :] for k in range(n_slabs)], axis=-1)
         for b in range(batch)], axis=0).astype(bf16)
    x1_ref[...] = (x + _dot(mixed, wout_ref[...])).reshape(batch, steps, D_MODEL)

    xp_scr[0:hist_rows, :] = xp_scr[rows:rows + hist_rows, :]

    @pl.when(c == pl.num_programs(0) - 1)
    def _():
        hlast_ref[...] = h_scr[...]
        histout_ref[...] = xp_scr[0:hist_rows, :]


def _const_spec(shape):
    return pl.BlockSpec(shape, lambda c: (0,) * len(shape), pipeline_mode=pl.Buffered(1))


def _perm_pitch(steps):
    tiles = steps // SUBLANES
    return SUBLANES * (tiles + 1 if tiles % 2 == 0 else tiles + 2)


def _mixer(x, h0, hist, start_pos, wts):
    batch, length, _ = x.shape
    assert MIXER_ROWS % batch == 0
    steps = MIXER_ROWS // batch
    assert length % steps == 0 and steps % SUBLANES == 0
    assert steps >= HIST_FRAMES and STATE_VREG_ELEMS % batch == 0
    hist_rows = HIST_FRAMES * batch
    kern = functools.partial(_mixer_kernel, batch=batch, steps=steps, start_pos=start_pos)
    out = jax.ShapeDtypeStruct
    x_spec = pl.BlockSpec((batch, steps, D_MODEL), lambda c: (0, c, 0))
    return pl.pallas_call(
        kern,
        grid=(length // steps,),
        in_specs=[x_spec, _const_spec(h0.shape), _const_spec(hist.shape)]
        + [_const_spec(w.shape) for w in wts],
        out_specs=(x_spec,
                   pl.BlockSpec(h0.shape, lambda c: (0, 0)),
                   pl.BlockSpec(hist.shape, lambda c: (0, 0))),
        out_shape=(out(x.shape, f32), out(h0.shape, f32), out(hist.shape, f32)),
        scratch_shapes=[pltpu.VMEM((MIXER_ROWS, S5_COLS), f32),
                        pltpu.VMEM((batch, S5_COLS), f32),
                        pltpu.VMEM((hist_rows + MIXER_ROWS, D_POOL), f32),
                        pltpu.VMEM((MIXER_ROWS, D_S5), f32),
                        pltpu.VMEM((D_MODEL // LANES, batch * _perm_pitch(steps), LANES), f32)],
        compiler_params=pltpu.CompilerParams(dimension_semantics=("arbitrary",),
                                             vmem_limit_bytes=VMEM_LIMIT_BYTES),
        name="mixer",
    )(x, h0, hist, *wts)


def _ffn_kernel(x_ref, p_ref, gmlp_ref, wup_ref, wdown_ref, gple_ref, wgate_ref, wproj_ref,
                gfinal_ref, o_ref):
    x = x_ref[...]
    xn = _rmsnorm(x, gmlp_ref[...]).astype(bf16)
    acc = x
    for f in range(D_FF // FFN_CHUNK):
        up = _dot(xn, wup_ref[:, f * FFN_CHUNK:(f + 1) * FFN_CHUNK])
        act = jnp.square(jnp.maximum(up, 0.0)).astype(bf16)
        acc = acc + _dot(act, wdown_ref[f * FFN_CHUNK:(f + 1) * FFN_CHUNK, :])
    gate = jax.nn.sigmoid(_dot(_rmsnorm(acc, gple_ref[...]).astype(bf16), wgate_ref[...]))
    x3 = acc + _dot(p_ref[...].astype(bf16), wproj_ref[...]) * gate
    o_ref[...] = _rmsnorm(x3, gfinal_ref[...])


def _ffn(x1, p, wts):
    n_rows = x1.shape[0]
    assert n_rows % FFN_ROWS == 0
    gmlp, wup, wdown, gple, wgate, wproj, gfinal = wts
    return pl.pallas_call(
        _ffn_kernel,
        grid=(n_rows // FFN_ROWS,),
        in_specs=[pl.BlockSpec((FFN_ROWS, D_MODEL), lambda i: (i, 0)),
                  pl.BlockSpec((FFN_ROWS, D_PLE), lambda i: (i, 0))] + [_const_spec(w.shape) for w in wts],
        out_specs=pl.BlockSpec((FFN_ROWS, D_MODEL), lambda i: (i, 0)),
        out_shape=jax.ShapeDtypeStruct((n_rows, D_MODEL), f32),
        compiler_params=pltpu.CompilerParams(dimension_semantics=("arbitrary",),
                                             vmem_limit_bytes=VMEM_LIMIT_BYTES),
        name="ffn",
    )(x1, p, *wts)


def _pack_state(re, im):
    b = re.shape[0]
    return jnp.concatenate([re.reshape(b, N_S5_BLOCKS, S5_BLOCK_STATES),
                            im.reshape(b, N_S5_BLOCKS, S5_BLOCK_STATES)], axis=2).reshape(b, S5_COLS)


def _unpack_state(h):
    b = h.shape[0]
    h = h.reshape(b, N_S5_BLOCKS, 2, S5_BLOCK_STATES)
    return (h[:, :, 0].reshape(b, N_S5_GROUPS, S5_STATE), h[:, :, 1].reshape(b, N_S5_GROUPS, S5_STATE))


def _pack_hist(hist):
    b = hist.shape[0]
    h = jnp.pad(jnp.transpose(hist, (1, 0, 2)), ((1, 0), (0, 0), (0, 0)))
    return h.reshape(HIST_FRAMES * b, D_POOL)


def _unpack_hist(h, b):
    return jnp.transpose(h.reshape(HIST_FRAMES, b, D_POOL)[1:], (1, 0, 2))


def kernel(x_prompt, x_sample, state_s5_re, state_s5_im, state_pool, p_prompt, p_sample, g_mix_norm, w_in, lambda_re, lambda_im, log_dt, b_re, b_im, c_re, c_im, d_skip, w_glu, w_pool, pool_scale, g_s5_out, g_pool_out, w_out, g_mlp_norm, w_up, w_down, g_ple_norm, w_ple_gate, w_ple_proj, g_final):
    assert x_prompt.shape[2] == D_MODEL and w_in.shape[0] == 1
    bsz_p, bsz_s = x_prompt.shape[0], x_sample.shape[0]
    row = lambda v: v.reshape(1, -1)

    a_re, a_im, bb_re, bb_im = _s5_prep(lambda_re[0], lambda_im[0], log_dt[0], b_re[0], b_im[0])
    blk = lambda w: w.reshape(N_S5_BLOCKS, 8, *w.shape[1:])
    bblk = jnp.concatenate([_block_diag(blk(bb_re)), _block_diag(blk(bb_im))], axis=2).astype(bf16)
    c_t = lambda w: blk(jnp.transpose(w, (0, 2, 1)))
    cblk = jnp.concatenate([_block_diag(c_t(c_re[0])), _block_diag(c_t(-c_im[0]))], axis=1).astype(bf16)

    mixer_wts = (row(g_mix_norm[0]), w_in[0].astype(bf16), row(a_re), row(a_im), bblk, cblk,
                 row(d_skip[0]), w_glu[0].astype(bf16), w_pool[0].astype(bf16), row(pool_scale[0]),
                 row(g_s5_out[0]), row(g_pool_out[0]), w_out[0].astype(bf16))
    ffn_wts = (row(g_mlp_norm[0]), w_up[0].astype(bf16), w_down[0].astype(bf16), row(g_ple_norm[0]),
               w_ple_gate[0].astype(bf16), w_ple_proj[0].astype(bf16), row(g_final))

    def run(x, p, h0_re, h0_im, hist, start_pos):
        b, length, _ = x.shape
        x1, h_last, hist_new = _mixer(x, _pack_state(h0_re, h0_im), _pack_hist(hist), start_pos, mixer_wts)
        y = _ffn(x1.reshape(b * length, D_MODEL), p.reshape(b * length, D_PLE), ffn_wts)
        re, im = _unpack_state(h_last)
        return y.reshape(b, length, D_MODEL), re[None], im[None], _unpack_hist(hist_new, b)[None]

    zeros_state = jnp.zeros((bsz_p, N_S5_GROUPS, S5_STATE), f32)
    y_p, re_p, im_p, pool_p = run(x_prompt, p_prompt[0], zeros_state, zeros_state,
                                  jnp.zeros((bsz_p, POOL_HIST, D_POOL), f32), 0)
    y_s, re_s, im_s, pool_s = run(x_sample, p_sample[0], state_s5_re[0], state_s5_im[0],
                                  state_pool[0], PAST_LEN)
    return (y_p, y_s, re_p, im_p, pool_p, re_s, im_s, pool_s)
```

```python
import functools
import math

import jax
import jax.numpy as jnp
from jax import lax
from jax.experimental import pallas as pl
from jax.experimental.pallas import tpu as pltpu

D_MODEL = 1024
D_S5 = 512
D_POOL = 512
S5_GROUP = 16
N_S5_GROUPS = 32
S5_STATE = 64
POOL_WINDOWS = (2, 4, 8, 16)
POOL_GROUP = 128
POOL_HIST = 15
HIST_FRAMES = POOL_HIST + 1
D_FF = 4096
D_PLE = 256
EPS = 1e-6
PAST_LEN = 1024

N_S5_BLOCKS = 4
S5_BLOCK_STATES = 512
S5_COLS = 2 * N_S5_GROUPS * S5_STATE
SUBLANES, LANES = 8, 128
STATE_VREG_ELEMS = 4 * SUBLANES * LANES

MIXER_ROWS = 512
FFN_ROWS = 512
FFN_CHUNK = 1024
VMEM_LIMIT_BYTES = 56 * 1024 * 1024

f32 = jnp.float32
bf16 = jnp.bfloat16


def _rmsnorm(x, g):
    y = x * lax.rsqrt(jnp.mean(x * x, axis=-1, keepdims=True) + EPS)
    return y * g


def _dot(a, b):
    return jnp.dot(a, b, preferred_element_type=f32)


def _s5_prep_kernel(lre_ref, lim_ref, logdt_ref, bre_ref, bim_ref, cre_ref, cim_ref,
                    a2_ref, w_ref, d_ref):
    lre = lre_ref[...]
    lim = lim_ref[...]
    dt = jnp.exp(logdt_ref[...])
    mag = jnp.exp(lre * dt)
    ang = lim * dt
    are = mag * jnp.cos(ang)
    aim = mag * jnp.sin(ang)
    a2re = are * are - aim * aim
    a2im = 2.0 * are * aim
    a2_ref[0] = a2re
    a2_ref[1] = a2im
    nr = are - 1.0
    ni = aim
    den = lre * lre + lim * lim
    kre = ((nr * lre + ni * lim) / den)[:, None, :]
    kim = ((ni * lre - nr * lim) / den)[:, None, :]

    def cmul(xr, xi, yr, yi):
        return xr * yr - xi * yi, xr * yi + xi * yr

    flat = lambda v: v.reshape(N_S5_GROUPS * S5_GROUP, S5_STATE)
    ar3, ai3 = are[:, None, :], aim[:, None, :]
    bbre, bbim = cmul(kre, kim, bre_ref[...], bim_ref[...])
    abre, abim = cmul(ar3, ai3, bbre, bbim)
    c_re, c_im = cre_ref[...], cim_ref[...]
    care, caim = cmul(ar3, ai3, c_re, c_im)
    c2re, c2im = cmul(a2re[:, None, :], a2im[:, None, :], c_re, c_im)
    for i, v in enumerate((bbre, bbim, abre, abim, care, caim, c2re, c2im)):
        w_ref[i] = flat(v)

    def nt(u, v):
        return lax.dot_general(flat(u), flat(v), (((1,), (1,)), ((), ())),
                               precision=lax.Precision.HIGHEST, preferred_element_type=f32)

    d_ref[0] = nt(bbre, c_re) - nt(bbim, c_im)
    d_ref[1] = nt(abre, c_re) - nt(abim, c_im)


def _s5_prep(lambda_re, lambda_im, log_dt, b_re, b_im, c_re, c_im):
    g, p, h = b_re.shape
    out = jax.ShapeDtypeStruct
    return pl.pallas_call(
        _s5_prep_kernel,
        out_shape=(out((2, g, p), f32), out((8, g * h, p), f32), out((2, g * h, g * h), f32)),
        name="s5_prep",
    )(lambda_re, lambda_im, log_dt.reshape(g, 1),
      jnp.transpose(b_re, (0, 2, 1)), jnp.transpose(b_im, (0, 2, 1)), c_re, c_im)


def _block_diag(w):
    j, g, m, n = w.shape
    eye = jnp.eye(g, dtype=w.dtype)
    return (w[:, :, :, None, :] * eye[None, :, None, :, None]).reshape(j, g * m, g * n)


def _s5_weights(w, d):
    per_group = lambda v: v.reshape(N_S5_BLOCKS, 8, S5_GROUP, S5_STATE)
    rows = lambda v: _block_diag(per_group(v))
    cols = lambda v: _block_diag(jnp.swapaxes(per_group(v), 2, 3))
    bbre, bbim, abre, abim, care, caim, c2re, c2im = (w[i] for i in range(8))
    blocks = jnp.arange(N_S5_BLOCKS)
    same_group = jnp.kron(jnp.eye(8, dtype=f32), jnp.ones((S5_GROUP, S5_GROUP), f32))
    feed = lambda v: v.reshape(N_S5_BLOCKS, 128, N_S5_BLOCKS, 128)[blocks, :, blocks, :] * same_group
    d0, d1 = feed(d[0]), feed(d[1])
    w1 = jnp.concatenate([jnp.concatenate([rows(abre), rows(abim), d0, d1], axis=2),
                          jnp.concatenate([rows(bbre), rows(bbim), jnp.zeros_like(d0), d0], axis=2)],
                         axis=1)
    w2 = jnp.concatenate([jnp.concatenate([cols(care), cols(c2re)], axis=2),
                          jnp.concatenate([-cols(caim), -cols(c2im)], axis=2)], axis=1)
    return w1.astype(bf16), w2.astype(bf16)


def _mixer_kernel(x_ref, h0re_ref, h0im_ref, hist_ref, gmix_ref, win_ref, a2re_ref, a2im_ref, w1_ref, w2_ref,
                  dskip_ref, wglu_ref, wpool_ref, pscale_ref, gs5_ref, gpool_ref, wout_ref,
                  x1_ref, hre_ref, him_ref, histout_ref,
                  v_scr, h_scr, xp_scr, zs_scr, perm_scr, *, batch, steps, start_pos):
    c = pl.program_id(0)
    rows = batch * steps
    half = rows // 2
    hist_rows = HIST_FRAMES * batch
    pitch = _perm_pitch(steps)
    n_slabs = D_MODEL // LANES

    def state_cols(j):
        lo = j * 2 * S5_BLOCK_STATES
        return slice(lo, lo + S5_BLOCK_STATES), slice(lo + S5_BLOCK_STATES, lo + 2 * S5_BLOCK_STATES)

    def block_cols(j):
        return slice(j * S5_BLOCK_STATES, (j + 1) * S5_BLOCK_STATES)

    @pl.when(c == 0)
    def _():
        for j in range(N_S5_BLOCKS):
            re_cols, im_cols = state_cols(j)
            h_scr[:, re_cols] = h0re_ref[:, block_cols(j)]
            h_scr[:, im_cols] = h0im_ref[:, block_cols(j)]
        xp_scr[0:hist_rows, :] = hist_ref[...]

    x = x_ref[...].reshape(rows, D_MODEL)
    z = _dot(_rmsnorm(x, gmix_ref[...]).astype(bf16), win_ref[...])

    for k in range(n_slabs):
        for b in range(batch):
            perm_scr[k, b * pitch:b * pitch + steps, :] = z[b * steps:(b + 1) * steps,
                                                            k * LANES:(k + 1) * LANES]
    for t in range(steps):
        q0 = (t // 2) * batch
        for k in range(n_slabs):
            v = perm_scr[k, pl.ds(t, batch, stride=pitch), :]
            if k * LANES < D_S5:
                zs_scr[t % 2, q0:q0 + batch, k * LANES:(k + 1) * LANES] = v
            else:
                xp_scr[hist_rows + t * batch:hist_rows + (t + 1) * batch,
                       k * LANES - D_S5:(k + 1) * LANES - D_S5] = v
    z_even, z_odd = zs_scr[0], zs_scr[1]

    feeds = []
    for j in range(N_S5_BLOCKS):
        ch = slice(j * 128, (j + 1) * 128)
        u = jnp.concatenate([z_even[:, ch], z_odd[:, ch]], axis=-1).astype(bf16)
        r = _dot(u, w1_ref[j])
        v_scr[:, j * 1024:(j + 1) * 1024] = r[:, :1024]
        feeds.append(r[:, 1024:])

    width = STATE_VREG_ELEMS // batch
    for k in range(N_S5_GROUPS * S5_STATE // width):
        j, o = divmod(k * width, S5_BLOCK_STATES)
        cre = j * 2 * S5_BLOCK_STATES + o
        cim = cre + S5_BLOCK_STATES
        ar = jnp.broadcast_to(a2re_ref[:, k * width:(k + 1) * width], (batch, width))
        ai = jnp.broadcast_to(a2im_ref[:, k * width:(k + 1) * width], (batch, width))

        hr, hi = h_scr[:, cre:cre + width], h_scr[:, cim:cim + width]
        for q in range(steps // 2):
            r0 = q * batch
            vr = v_scr[r0:r0 + batch, cre:cre + width]
            vi = v_scr[r0:r0 + batch, cim:cim + width]
            v_scr[r0:r0 + batch, cre:cre + width] = hr
            v_scr[r0:r0 + batch, cim:cim + width] = hi
            hr, hi = ar * hr - ai * hi + vr, ar * hi + ai * hr + vi
        h_scr[:, cre:cre + width] = hr
        h_scr[:, cim:cim + width] = hi

    y_even, y_odd = [], []
    for j in range(N_S5_BLOCKS):
        yy = _dot(v_scr[:, j * 1024:(j + 1) * 1024].astype(bf16), w2_ref[j]) + feeds[j]
        y_even.append(yy[:, :128])
        y_odd.append(yy[:, 128:])
    y = jnp.concatenate([jnp.concatenate(y_even, axis=-1), jnp.concatenate(y_odd, axis=-1)], axis=0)
    zs = jnp.concatenate([z_even, z_odd], axis=0)
    y = jax.nn.gelu(y + dskip_ref[...] * zs)
    y = y * jax.nn.sigmoid(_dot(y.astype(bf16), wglu_ref[...]))
    y_s5 = _rmsnorm(y, gs5_ref[...])

    t_idx = lax.shift_right_logical(lax.broadcasted_iota(jnp.int32, (rows, 1), 0),
                                    int(math.log2(batch)))
    pos1 = start_pos + c * steps + t_idx + 1
    outs = []
    for g, w in enumerate(POOL_WINDOWS):
        ext = xp_scr[:, g * POOL_GROUP:(g + 1) * POOL_GROUP]
        s = ext
        k = 1
        while k < w:
            s = s[k * batch:] + s[:-k * batch]
            k *= 2
        win = s[(HIST_FRAMES - (w - 1)) * batch:]
        cnt = jnp.minimum(pos1, w).astype(f32)
        pooled = win * (1.0 / cnt) - ext[hist_rows:]
        outs.append(_dot(pooled.astype(bf16), wpool_ref[g]))
    y_pool = _rmsnorm(jnp.concatenate(outs, axis=-1) * pscale_ref[...], gpool_ref[...])

    for t in range(steps):
        src = (t % 2) * half + (t // 2) * batch
        for k in range(n_slabs):
            if k * LANES < D_S5:
                v = y_s5[src:src + batch, k * LANES:(k + 1) * LANES]
            else:
                v = y_pool[t * batch:(t + 1) * batch, k * LANES - D_S5:(k + 1) * LANES - D_S5]
            perm_scr[k, pl.ds(t, batch, stride=pitch), :] = v
    mixed = jnp.concatenate(
        [jnp.concatenate([perm_scr[k, b * pitch:b * pitch + steps, :] for k in range(n_slabs)], axis=-1)
         for b in range(batch)], axis=0).astype(bf16)
    x1_ref[...] = (x + _dot(mixed, wout_ref[...])).reshape(batch, steps, D_MODEL)

    xp_scr[0:hist_rows, :] = xp_scr[rows:rows + hist_rows, :]

    @pl.when(c == pl.num_programs(0) - 1)
    def _():
        for j in range(N_S5_BLOCKS):
            re_cols, im_cols = state_cols(j)
            hre_ref[:, block_cols(j)] = h_scr[:, re_cols]
            him_ref[:, block_cols(j)] = h_scr[:, im_cols]
        histout_ref[...] = xp_scr[0:hist_rows, :]


def _const_spec(shape):
    return pl.BlockSpec(shape, lambda c: (0,) * len(shape), pipeline_mode=pl.Buffered(1))


def _perm_pitch(steps):
    tiles = steps // SUBLANES
    return SUBLANES * (tiles + 1 if tiles % 2 == 0 else tiles + 2)


def _mixer(x, h0re, h0im, hist, start_pos, wts):
    batch, length, _ = x.shape
    assert MIXER_ROWS % batch == 0
    steps = MIXER_ROWS // batch
    assert length % steps == 0 and steps % SUBLANES == 0
    assert steps >= HIST_FRAMES and steps % 2 == 0 and STATE_VREG_ELEMS % batch == 0
    hist_rows = HIST_FRAMES * batch
    kern = functools.partial(_mixer_kernel, batch=batch, steps=steps, start_pos=start_pos)
    out = jax.ShapeDtypeStruct
    x_spec = pl.BlockSpec((batch, steps, D_MODEL), lambda c: (0, c, 0))
    return pl.pallas_call(
        kern,
        grid=(length // steps,),
        in_specs=[x_spec, _const_spec(h0re.shape), _const_spec(h0im.shape), _const_spec(hist.shape)]
        + [_const_spec(w.shape) for w in wts],
        out_specs=(x_spec,
                   pl.BlockSpec(h0re.shape, lambda c: (0, 0)),
                   pl.BlockSpec(h0im.shape, lambda c: (0, 0)),
                   pl.BlockSpec(hist.shape, lambda c: (0, 0))),
        out_shape=(out(x.shape, f32), out(h0re.shape, f32), out(h0im.shape, f32), out(hist.shape, f32)),
        scratch_shapes=[pltpu.VMEM((MIXER_ROWS // 2, S5_COLS), f32),
                        pltpu.VMEM((batch, S5_COLS), f32),
                        pltpu.VMEM((hist_rows + MIXER_ROWS, D_POOL), f32),
                        pltpu.VMEM((2, MIXER_ROWS // 2, D_S5), f32),
                        pltpu.VMEM((D_MODEL // LANES, batch * _perm_pitch(steps), LANES), f32)],
        compiler_params=pltpu.CompilerParams(dimension_semantics=("arbitrary",),
                                             vmem_limit_bytes=VMEM_LIMIT_BYTES),
        name="mixer",
    )(x, h0re, h0im, hist, *wts)


def _ffn_kernel(x_ref, p_ref, gmlp_ref, wup_ref, wdown_ref, gple_ref, wgate_ref, wproj_ref,
                gfinal_ref, o_ref):
    x = x_ref[...]
    xn = _rmsnorm(x, gmlp_ref[...]).astype(bf16)
    acc = x
    for f in range(D_FF // FFN_CHUNK):
        up = _dot(xn, wup_ref[:, f * FFN_CHUNK:(f + 1) * FFN_CHUNK])
        act = jnp.square(jnp.maximum(up, 0.0)).astype(bf16)
        acc = acc + _dot(act, wdown_ref[f * FFN_CHUNK:(f + 1) * FFN_CHUNK, :])
    gate = jax.nn.sigmoid(_dot(_rmsnorm(acc, gple_ref[...]).astype(bf16), wgate_ref[...]))
    x3 = acc + _dot(p_ref[...].astype(bf16), wproj_ref[...]) * gate
    o_ref[...] = _rmsnorm(x3, gfinal_ref[...])


def _ffn(x1, p, wts):
    n_rows = x1.shape[0]
    assert n_rows % FFN_ROWS == 0
    gmlp, wup, wdown, gple, wgate, wproj, gfinal = wts
    return pl.pallas_call(
        _ffn_kernel,
        grid=(n_rows // FFN_ROWS,),
        in_specs=[pl.BlockSpec((FFN_ROWS, D_MODEL), lambda i: (i, 0)),
                  pl.BlockSpec((FFN_ROWS, D_PLE), lambda i: (i, 0))] + [_const_spec(w.shape) for w in wts],
        out_specs=pl.BlockSpec((FFN_ROWS, D_MODEL), lambda i: (i, 0)),
        out_shape=jax.ShapeDtypeStruct((n_rows, D_MODEL), f32),
        compiler_params=pltpu.CompilerParams(dimension_semantics=("arbitrary",),
                                             vmem_limit_bytes=VMEM_LIMIT_BYTES),
        name="ffn",
    )(x1, p, *wts)


def _pack_hist(hist):
    b = hist.shape[0]
    h = jnp.pad(jnp.transpose(hist, (1, 0, 2)), ((1, 0), (0, 0), (0, 0)))
    return h.reshape(HIST_FRAMES * b, D_POOL)


def _unpack_hist(h, b):
    return jnp.transpose(h.reshape(HIST_FRAMES, b, D_POOL)[1:], (1, 0, 2))


def kernel(x_prompt, x_sample, state_s5_re, state_s5_im, state_pool, p_prompt, p_sample, g_mix_norm, w_in, lambda_re, lambda_im, log_dt, b_re, b_im, c_re, c_im, d_skip, w_glu, w_pool, pool_scale, g_s5_out, g_pool_out, w_out, g_mlp_norm, w_up, w_down, g_ple_norm, w_ple_gate, w_ple_proj, g_final):
    assert x_prompt.shape[2] == D_MODEL and w_in.shape[0] == 1
    bsz_p, bsz_s = x_prompt.shape[0], x_sample.shape[0]
    row = lambda v: v.reshape(1, -1)

    a2, s5_w, s5_d = _s5_prep(lambda_re[0], lambda_im[0], log_dt[0], b_re[0], b_im[0], c_re[0], c_im[0])
    w1, w2 = _s5_weights(s5_w, s5_d)

    mixer_wts = (row(g_mix_norm[0]), w_in[0].astype(bf16), row(a2[0]), row(a2[1]), w1, w2,
                 row(d_skip[0]), w_glu[0].astype(bf16), w_pool[0].astype(bf16), row(pool_scale[0]),
                 row(g_s5_out[0]), row(g_pool_out[0]), w_out[0].astype(bf16))
    ffn_wts = (row(g_mlp_norm[0]), w_up[0].astype(bf16), w_down[0].astype(bf16), row(g_ple_norm[0]),
               w_ple_gate[0].astype(bf16), w_ple_proj[0].astype(bf16), row(g_final))

    def run(x, p, h0_re, h0_im, hist, start_pos):
        b, length, _ = x.shape
        flat = lambda h: h.reshape(b, N_S5_GROUPS * S5_STATE)
        x1, re, im, hist_new = _mixer(x, flat(h0_re), flat(h0_im), _pack_hist(hist), start_pos, mixer_wts)
        state = lambda h: h.reshape(1, b, N_S5_GROUPS, S5_STATE)
        y = _ffn(x1.reshape(b * length, D_MODEL), p.reshape(b * length, D_PLE), ffn_wts)
        return y.reshape(x.shape), state(re), state(im), _unpack_hist(hist_new, b)[None]

    zeros_state = jnp.zeros((bsz_p, N_S5_GROUPS, S5_STATE), f32)
    y_p, re_p, im_p, pool_p = run(x_prompt, p_prompt[0], zeros_state, zeros_state,
                                  jnp.zeros((bsz_p, POOL_HIST, D_POOL), f32), 0)
    y_s, re_s, im_s, pool_s = run(x_sample, p_sample[0], state_s5_re[0], state_s5_im[0],
                                  state_pool[0], PAST_LEN)
    return (y_p, y_s, re_p, im_p, pool_p, re_s, im_s, pool_s)
```

```python
import functools
import math

import jax
import jax.numpy as jnp
from jax import lax
from jax.experimental import pallas as pl
from jax.experimental.pallas import tpu as pltpu

D_MODEL = 1024
D_S5 = 512
D_POOL = 512
S5_GROUP = 16
N_S5_GROUPS = 32
S5_STATE = 64
POOL_WINDOWS = (2, 4, 8, 16)
POOL_GROUP = 128
POOL_HIST = 15
HIST_FRAMES = POOL_HIST + 1
D_FF = 4096
D_PLE = 256
EPS = 1e-6
PAST_LEN = 1024

N_S5_BLOCKS = 4
S5_BLOCK_STATES = 512
S5_COLS = 2 * N_S5_GROUPS * S5_STATE
SUBLANES, LANES = 8, 128
STATE_VREG_ELEMS = 4 * SUBLANES * LANES

MIXER_ROWS = 512
FFN_ROWS = 512
FFN_CHUNK = 1024
VMEM_LIMIT_BYTES = 56 * 1024 * 1024

f32 = jnp.float32
bf16 = jnp.bfloat16


def _rmsnorm(x, g):
    y = x * lax.rsqrt(jnp.mean(x * x, axis=-1, keepdims=True) + EPS)
    return y * g


def _dot(a, b):
    return jnp.dot(a, b, preferred_element_type=f32)


def _s5_prep_kernel(lre_ref, lim_ref, logdt_ref, bre_ref, bim_ref, cre_ref, cim_ref,
                    a2_ref, w_ref, d_ref):
    lre = lre_ref[...]
    lim = lim_ref[...]
    dt = jnp.exp(logdt_ref[...])
    mag = jnp.exp(lre * dt)
    ang = lim * dt
    are = mag * jnp.cos(ang)
    aim = mag * jnp.sin(ang)
    a2re = are * are - aim * aim
    a2im = 2.0 * are * aim
    a2_ref[0] = a2re
    a2_ref[1] = a2im
    nr = are - 1.0
    ni = aim
    den = lre * lre + lim * lim
    kre = ((nr * lre + ni * lim) / den)[:, None, :]
    kim = ((ni * lre - nr * lim) / den)[:, None, :]

    def cmul(xr, xi, yr, yi):
        return xr * yr - xi * yi, xr * yi + xi * yr

    flat = lambda v: v.reshape(N_S5_GROUPS * S5_GROUP, S5_STATE)
    ar3, ai3 = are[:, None, :], aim[:, None, :]
    bbre, bbim = cmul(kre, kim, bre_ref[...], bim_ref[...])
    abre, abim = cmul(ar3, ai3, bbre, bbim)
    c_re, c_im = cre_ref[...], cim_ref[...]
    care, caim = cmul(ar3, ai3, c_re, c_im)
    c2re, c2im = cmul(a2re[:, None, :], a2im[:, None, :], c_re, c_im)
    for i, v in enumerate((bbre, bbim, abre, abim, care, caim, c2re, c2im)):
        w_ref[i] = flat(v)

    def nt(u, v):
        return lax.dot_general(flat(u), flat(v), (((1,), (1,)), ((), ())),
                               precision=lax.Precision.HIGHEST, preferred_element_type=f32)

    d_ref[0] = nt(bbre, c_re) - nt(bbim, c_im)
    d_ref[1] = nt(abre, c_re) - nt(abim, c_im)


def _s5_prep(lambda_re, lambda_im, log_dt, b_re, b_im, c_re, c_im):
    g, p, h = b_re.shape
    out = jax.ShapeDtypeStruct
    return pl.pallas_call(
        _s5_prep_kernel,
        out_shape=(out((2, g, p), f32), out((8, g * h, p), f32), out((2, g * h, g * h), f32)),
        name="s5_prep",
    )(lambda_re, lambda_im, log_dt.reshape(g, 1),
      jnp.transpose(b_re, (0, 2, 1)), jnp.transpose(b_im, (0, 2, 1)), c_re, c_im)


def _block_diag(w):
    j, g, m, n = w.shape
    eye = jnp.eye(g, dtype=w.dtype)
    return (w[:, :, :, None, :] * eye[None, :, None, :, None]).reshape(j, g * m, g * n)


def _s5_weights(w, d):
    per_group = lambda v: v.reshape(N_S5_BLOCKS, 8, S5_GROUP, S5_STATE)
    rows = lambda v: _block_diag(per_group(v))
    cols = lambda v: _block_diag(jnp.swapaxes(per_group(v), 2, 3))
    bbre, bbim, abre, abim, care, caim, c2re, c2im = (w[i] for i in range(8))
    blocks = jnp.arange(N_S5_BLOCKS)
    same_group = jnp.kron(jnp.eye(8, dtype=f32), jnp.ones((S5_GROUP, S5_GROUP), f32))
    feed = lambda v: v.reshape(N_S5_BLOCKS, 128, N_S5_BLOCKS, 128)[blocks, :, blocks, :] * same_group
    d0, d1 = feed(d[0]), feed(d[1])
    w1 = jnp.concatenate([jnp.concatenate([rows(abre), rows(abim), d0, d1], axis=2),
                          jnp.concatenate([rows(bbre), rows(bbim), jnp.zeros_like(d0), d0], axis=2)],
                         axis=1)
    w2 = jnp.concatenate([jnp.concatenate([cols(care), cols(c2re)], axis=2),
                          jnp.concatenate([-cols(caim), -cols(c2im)], axis=2)], axis=1)
    return w1.astype(bf16), w2.astype(bf16)


def _mixer_kernel(*refs, batch, steps, start_pos, n_cast):
    x_ref, h0re_ref, h0im_ref, hist_ref = refs[:4]
    (gmix_ref, win_ref, a2re_ref, a2im_ref, w1_ref, w2_ref, dskip_ref, wglu_ref, wpool_ref,
     pscale_ref, gs5_ref, gpool_ref, wout_ref) = refs[4:4 + N_MIXER_W]
    cast_in = refs[4 + N_MIXER_W:4 + N_MIXER_W + n_cast]
    outs = refs[4 + N_MIXER_W + n_cast:]
    x1_ref, hre_ref, him_ref, histout_ref = outs[:4]
    cast_out = outs[4:4 + n_cast]
    v_scr, h_scr, xp_scr, zs_scr, perm_scr = outs[4 + n_cast:]
    c = pl.program_id(0)

    for src, dst in zip(cast_in, cast_out):
        dst[...] = src[...].astype(bf16)

    rows = batch * steps
    half = rows // 2
    hist_rows = HIST_FRAMES * batch
    pitch = _perm_pitch(steps)
    n_slabs = D_MODEL // LANES

    def state_cols(j):
        lo = j * 2 * S5_BLOCK_STATES
        return slice(lo, lo + S5_BLOCK_STATES), slice(lo + S5_BLOCK_STATES, lo + 2 * S5_BLOCK_STATES)

    def block_cols(j):
        return slice(j * S5_BLOCK_STATES, (j + 1) * S5_BLOCK_STATES)

    @pl.when(c == 0)
    def _():
        for j in range(N_S5_BLOCKS):
            re_cols, im_cols = state_cols(j)
            h_scr[:, re_cols] = h0re_ref[:, block_cols(j)]
            h_scr[:, im_cols] = h0im_ref[:, block_cols(j)]
        xp_scr[0:hist_rows, :] = hist_ref[...]

    x = x_ref[...].reshape(rows, D_MODEL)
    z = _dot(_rmsnorm(x, gmix_ref[...]).astype(bf16), win_ref[...])

    for k in range(n_slabs):
        for b in range(batch):
            perm_scr[k, b * pitch:b * pitch + steps, :] = z[b * steps:(b + 1) * steps,
                                                            k * LANES:(k + 1) * LANES]
    for t in range(steps):
        q0 = (t // 2) * batch
        for k in range(n_slabs):
            v = perm_scr[k, pl.ds(t, batch, stride=pitch), :]
            if k * LANES < D_S5:
                zs_scr[t % 2, q0:q0 + batch, k * LANES:(k + 1) * LANES] = v
            else:
                xp_scr[hist_rows + t * batch:hist_rows + (t + 1) * batch,
                       k * LANES - D_S5:(k + 1) * LANES - D_S5] = v
    z_even, z_odd = zs_scr[0], zs_scr[1]

    feeds = []
    for j in range(N_S5_BLOCKS):
        ch = slice(j * 128, (j + 1) * 128)
        u = jnp.concatenate([z_even[:, ch], z_odd[:, ch]], axis=-1).astype(bf16)
        r = _dot(u, w1_ref[j])
        v_scr[:, j * 1024:(j + 1) * 1024] = r[:, :1024]
        feeds.append(r[:, 1024:])

    width = STATE_VREG_ELEMS // batch
    for k in range(N_S5_GROUPS * S5_STATE // width):
        j, o = divmod(k * width, S5_BLOCK_STATES)
        cre = j * 2 * S5_BLOCK_STATES + o
        cim = cre + S5_BLOCK_STATES
        ar = jnp.broadcast_to(a2re_ref[:, k * width:(k + 1) * width], (batch, width))
        ai = jnp.broadcast_to(a2im_ref[:, k * width:(k + 1) * width], (batch, width))

        hr, hi = h_scr[:, cre:cre + width], h_scr[:, cim:cim + width]
        for q in range(steps // 2):
            r0 = q * batch
            vr = v_scr[r0:r0 + batch, cre:cre + width]
            vi = v_scr[r0:r0 + batch, cim:cim + width]
            v_scr[r0:r0 + batch, cre:cre + width] = hr
            v_scr[r0:r0 + batch, cim:cim + width] = hi
            hr, hi = ar * hr - ai * hi + vr, ar * hi + ai * hr + vi
        h_scr[:, cre:cre + width] = hr
        h_scr[:, cim:cim + width] = hi

    y_even, y_odd = [], []
    for j in range(N_S5_BLOCKS):
        yy = _dot(v_scr[:, j * 1024:(j + 1) * 1024].astype(bf16), w2_ref[j]) + feeds[j]
        y_even.append(yy[:, :128])
        y_odd.append(yy[:, 128:])
    y = jnp.concatenate([jnp.concatenate(y_even, axis=-1), jnp.concatenate(y_odd, axis=-1)], axis=0)
    zs = jnp.concatenate([z_even, z_odd], axis=0)
    y = jax.nn.gelu(y + dskip_ref[...] * zs)
    y = y * jax.nn.sigmoid(_dot(y.astype(bf16), wglu_ref[...]))
    y_s5 = _rmsnorm(y, gs5_ref[...])

    t_idx = lax.shift_right_logical(lax.broadcasted_iota(jnp.int32, (rows, 1), 0),
                                    int(math.log2(batch)))
    pos1 = start_pos + c * steps + t_idx + 1
    outs = []
    for g, w in enumerate(POOL_WINDOWS):
        ext = xp_scr[:, g * POOL_GROUP:(g + 1) * POOL_GROUP]
        s = ext
        k = 1
        while k < w:
            s = s[k * batch:] + s[:-k * batch]
            k *= 2
        win = s[(HIST_FRAMES - (w - 1)) * batch:]
        cnt = jnp.minimum(pos1, w).astype(f32)
        pooled = win * (1.0 / cnt) - ext[hist_rows:]
        outs.append(_dot(pooled.astype(bf16), wpool_ref[g]))
    y_pool = _rmsnorm(jnp.concatenate(outs, axis=-1) * pscale_ref[...], gpool_ref[...])

    for t in range(steps):
        src = (t % 2) * half + (t // 2) * batch
        for k in range(n_slabs):
            if k * LANES < D_S5:
                v = y_s5[src:src + batch, k * LANES:(k + 1) * LANES]
            else:
                v = y_pool[t * batch:(t + 1) * batch, k * LANES - D_S5:(k + 1) * LANES - D_S5]
            perm_scr[k, pl.ds(t, batch, stride=pitch), :] = v
    mixed = jnp.concatenate(
        [jnp.concatenate([perm_scr[k, b * pitch:b * pitch + steps, :] for k in range(n_slabs)], axis=-1)
         for b in range(batch)], axis=0).astype(bf16)
    x1_ref[...] = (x + _dot(mixed, wout_ref[...])).reshape(batch, steps, D_MODEL)

    xp_scr[0:hist_rows, :] = xp_scr[rows:rows + hist_rows, :]

    @pl.when(c == pl.num_programs(0) - 1)
    def _():
        for j in range(N_S5_BLOCKS):
            re_cols, im_cols = state_cols(j)
            hre_ref[:, block_cols(j)] = h_scr[:, re_cols]
            him_ref[:, block_cols(j)] = h_scr[:, im_cols]
        histout_ref[...] = xp_scr[0:hist_rows, :]


N_MIXER_W = 13


def _const_spec(shape):
    return pl.BlockSpec(shape, lambda c: (0,) * len(shape), pipeline_mode=pl.Buffered(1))


def _perm_pitch(steps):
    tiles = steps // SUBLANES
    return SUBLANES * (tiles + 1 if tiles % 2 == 0 else tiles + 2)


def _mixer(x, h0re, h0im, hist, start_pos, wts, to_bf16=()):
    batch, length, _ = x.shape
    assert MIXER_ROWS % batch == 0 and len(wts) == N_MIXER_W
    steps = MIXER_ROWS // batch
    assert length % steps == 0 and steps % SUBLANES == 0
    assert steps >= HIST_FRAMES and steps % 2 == 0 and STATE_VREG_ELEMS % batch == 0
    hist_rows = HIST_FRAMES * batch
    n_steps = length // steps
    kern = functools.partial(_mixer_kernel, batch=batch, steps=steps, start_pos=start_pos,
                             n_cast=len(to_bf16))
    out = jax.ShapeDtypeStruct
    x_spec = pl.BlockSpec((batch, steps, D_MODEL), lambda c: (0, c, 0))
    cast_specs = []
    for w, axis in to_bf16:
        size = w.shape[axis] // n_steps
        assert w.ndim == 2 and size * n_steps == w.shape[axis] and size % (LANES if axis else 2 * SUBLANES) == 0
        cast_specs.append(pl.BlockSpec((w.shape[0], size), lambda c: (0, c)) if axis else
                          pl.BlockSpec((size, w.shape[1]), lambda c: (c, 0)))
    return pl.pallas_call(
        kern,
        grid=(n_steps,),
        in_specs=[x_spec, _const_spec(h0re.shape), _const_spec(h0im.shape), _const_spec(hist.shape)]
        + [_const_spec(w.shape) for w in wts] + cast_specs,
        out_specs=(x_spec,
                   pl.BlockSpec(h0re.shape, lambda c: (0, 0)),
                   pl.BlockSpec(h0im.shape, lambda c: (0, 0)),
                   pl.BlockSpec(hist.shape, lambda c: (0, 0)), *cast_specs),
        out_shape=(out(x.shape, f32), out(h0re.shape, f32), out(h0im.shape, f32), out(hist.shape, f32),
                   *(out(w.shape, bf16) for w, _ in to_bf16)),
        scratch_shapes=[pltpu.VMEM((MIXER_ROWS // 2, S5_COLS), f32),
                        pltpu.VMEM((batch, S5_COLS), f32),
                        pltpu.VMEM((hist_rows + MIXER_ROWS, D_POOL), f32),
                        pltpu.VMEM((2, MIXER_ROWS // 2, D_S5), f32),
                        pltpu.VMEM((D_MODEL // LANES, batch * _perm_pitch(steps), LANES), f32)],
        compiler_params=pltpu.CompilerParams(dimension_semantics=("arbitrary",),
                                             vmem_limit_bytes=VMEM_LIMIT_BYTES),
        name="mixer",
    )(x, h0re, h0im, hist, *wts, *(w for w, _ in to_bf16))


def _ffn_kernel(x_ref, p_ref, gmlp_ref, wup_ref, wdown_ref, gple_ref, wgate_ref, wproj_ref,
                gfinal_ref, o_ref):
    x = x_ref[...]
    xn = _rmsnorm(x, gmlp_ref[...]).astype(bf16)
    acc = x
    for f in range(D_FF // FFN_CHUNK):
        up = _dot(xn, wup_ref[:, f * FFN_CHUNK:(f + 1) * FFN_CHUNK])
        act = jnp.square(jnp.maximum(up, 0.0)).astype(bf16)
        acc = acc + _dot(act, wdown_ref[f * FFN_CHUNK:(f + 1) * FFN_CHUNK, :])
    gate = jax.nn.sigmoid(_dot(_rmsnorm(acc, gple_ref[...]).astype(bf16), wgate_ref[...]))
    x3 = acc + _dot(p_ref[...].astype(bf16), wproj_ref[...]) * gate
    o_ref[...] = _rmsnorm(x3, gfinal_ref[...])


def _ffn(x1, p, wts):
    n_rows = x1.shape[0]
    assert n_rows % FFN_ROWS == 0
    gmlp, wup, wdown, gple, wgate, wproj, gfinal = wts
    return pl.pallas_call(
        _ffn_kernel,
        grid=(n_rows // FFN_ROWS,),
        in_specs=[pl.BlockSpec((FFN_ROWS, D_MODEL), lambda i: (i, 0)),
                  pl.BlockSpec((FFN_ROWS, D_PLE), lambda i: (i, 0))] + [_const_spec(w.shape) for w in wts],
        out_specs=pl.BlockSpec((FFN_ROWS, D_MODEL), lambda i: (i, 0)),
        out_shape=jax.ShapeDtypeStruct((n_rows, D_MODEL), f32),
        compiler_params=pltpu.CompilerParams(dimension_semantics=("arbitrary",),
                                             vmem_limit_bytes=VMEM_LIMIT_BYTES),
        name="ffn",
    )(x1, p, *wts)


def _pack_hist(hist):
    b = hist.shape[0]
    h = jnp.pad(jnp.transpose(hist, (1, 0, 2)), ((1, 0), (0, 0), (0, 0)))
    return h.reshape(HIST_FRAMES * b, D_POOL)


def _unpack_hist(h, b):
    return jnp.transpose(h.reshape(HIST_FRAMES, b, D_POOL)[1:], (1, 0, 2))


def kernel(x_prompt, x_sample, state_s5_re, state_s5_im, state_pool, p_prompt, p_sample, g_mix_norm, w_in, lambda_re, lambda_im, log_dt, b_re, b_im, c_re, c_im, d_skip, w_glu, w_pool, pool_scale, g_s5_out, g_pool_out, w_out, g_mlp_norm, w_up, w_down, g_ple_norm, w_ple_gate, w_ple_proj, g_final):
    assert x_prompt.shape[2] == D_MODEL and w_in.shape[0] == 1
    bsz_p, bsz_s = x_prompt.shape[0], x_sample.shape[0]
    row = lambda v: v.reshape(1, -1)

    a2, s5_w, s5_d = _s5_prep(lambda_re[0], lambda_im[0], log_dt[0], b_re[0], b_im[0], c_re[0], c_im[0])
    w1, w2 = _s5_weights(s5_w, s5_d)

    mixer_wts = (row(g_mix_norm[0]), w_in[0].astype(bf16), row(a2[0]), row(a2[1]), w1, w2,
                 row(d_skip[0]), w_glu[0].astype(bf16), w_pool[0].astype(bf16), row(pool_scale[0]),
                 row(g_s5_out[0]), row(g_pool_out[0]), w_out[0].astype(bf16))
    def mix(x, h0_re, h0_im, hist, start_pos, to_bf16=()):
        b = x.shape[0]
        flat = lambda h: h.reshape(b, N_S5_GROUPS * S5_STATE)
        x1, re, im, hist_new, *cast = _mixer(x, flat(h0_re), flat(h0_im), _pack_hist(hist), start_pos,
                                             mixer_wts, to_bf16)
        state = lambda h: h.reshape(1, b, N_S5_GROUPS, S5_STATE)
        return x1, state(re), state(im), _unpack_hist(hist_new, b)[None], cast

    def ffn(x1, p, ffn_wts):
        y = _ffn(x1.reshape(-1, D_MODEL), p.reshape(-1, D_PLE), ffn_wts)
        return y.reshape(x1.shape)

    zeros_state = jnp.zeros((bsz_p, N_S5_GROUPS, S5_STATE), f32)
    x1_p, re_p, im_p, pool_p, (wup, wdown, wgate) = mix(
        x_prompt, zeros_state, zeros_state, jnp.zeros((bsz_p, POOL_HIST, D_POOL), f32), 0,
        ((w_up[0], 1), (w_down[0], 0), (w_ple_gate[0], 0)))
    ffn_wts = (row(g_mlp_norm[0]), wup, wdown, row(g_ple_norm[0]), wgate, w_ple_proj[0].astype(bf16),
               row(g_final))
    y_p = ffn(x1_p, p_prompt[0], ffn_wts)
    x1_s, re_s, im_s, pool_s, _ = mix(x_sample, state_s5_re[0], state_s5_im[0], state_pool[0], PAST_LEN)
    y_s = ffn(x1_s, p_sample[0], ffn_wts)
    return (y_p, y_s, re_p, im_p, pool_p, re_s, im_s, pool_s)
```

```python
import functools
import math

import jax
import jax.numpy as jnp
from jax import lax
from jax.experimental import pallas as pl
from jax.experimental.pallas import tpu as pltpu

D_MODEL = 1024
D_S5 = 512
D_POOL = 512
S5_GROUP = 16
N_S5_GROUPS = 32
S5_STATE = 64
POOL_WINDOWS = (2, 4, 8, 16)
POOL_GROUP = 128
POOL_HIST = 15
HIST_FRAMES = POOL_HIST + 1
D_FF = 4096
D_PLE = 256
EPS = 1e-6
PAST_LEN = 1024

N_S5_BLOCKS = 4
S5_BLOCK_STATES = 512
S5_COLS = 2 * N_S5_GROUPS * S5_STATE
SUBLANES, LANES = 8, 128
STATE_VREG_ELEMS = 4 * SUBLANES * LANES

MIXER_ROWS = 512
FFN_ROWS = 1024
FFN_CHUNK = 1024
VMEM_LIMIT_BYTES = 56 * 1024 * 1024

f32 = jnp.float32
bf16 = jnp.bfloat16


def _rmsnorm(x, g):
    y = x * lax.rsqrt(jnp.mean(x * x, axis=-1, keepdims=True) + EPS)
    return y * g


def _dot(a, b):
    return jnp.dot(a, b, preferred_element_type=f32)


def _s5_prep_kernel(lre_ref, lim_ref, logdt_ref, bre_ref, bim_ref, cre_ref, cim_ref, tile_ref,
                    a2_ref, w1_ref, w2_ref):
    lre = lre_ref[...]
    lim = lim_ref[...]
    dt = jnp.exp(logdt_ref[...])
    mag = jnp.exp(lre * dt)
    ang = lim * dt
    are = mag * jnp.cos(ang)
    aim = mag * jnp.sin(ang)
    a2re = are * are - aim * aim
    a2im = 2.0 * are * aim
    a2_ref[0] = a2re
    a2_ref[1] = a2im
    nr = are - 1.0
    ni = aim
    den = lre * lre + lim * lim
    kre = ((nr * lre + ni * lim) / den)[:, None, :]
    kim = ((ni * lre - nr * lim) / den)[:, None, :]

    def cmul(xr, xi, yr, yi):
        return xr * yr - xi * yi, xr * yi + xi * yr

    flat = lambda v: v.reshape(N_S5_GROUPS * S5_GROUP, S5_STATE)
    ar3, ai3 = are[:, None, :], aim[:, None, :]
    bbre, bbim = cmul(kre, kim, bre_ref[...], bim_ref[...])
    abre, abim = cmul(ar3, ai3, bbre, bbim)
    c_re, c_im = cre_ref[...], cim_ref[...]
    care, caim = cmul(ar3, ai3, c_re, c_im)
    c2re, c2im = cmul(a2re[:, None, :], a2im[:, None, :], c_re, c_im)
    bbre, bbim, abre, abim, care, caim, c2re, c2im, c_re, c_im = map(
        flat, (bbre, bbim, abre, abim, care, caim, c2re, c2im, c_re, c_im))

    exact = dict(precision=lax.Precision.HIGHEST, preferred_element_type=f32)
    row_group = lax.shift_right_logical(lax.broadcasted_iota(jnp.int32, (128, 512), 0), 4)
    col_group = lax.shift_right_logical(lax.broadcasted_iota(jnp.int32, (128, 512), 1), 6)
    own_states = (row_group == col_group).astype(f32)
    same_group = (row_group[:, :128] ==
                  lax.shift_right_logical(lax.broadcasted_iota(jnp.int32, (128, 128), 1), 4)).astype(f32)

    for j in range(N_S5_BLOCKS):
        blk = slice(j * 128, (j + 1) * 128)

        def spread(v):
            return jnp.dot(v[blk], tile_ref[...], **exact) * own_states

        def feed(xr, xi):
            nt = lambda u, v: lax.dot_general(u[blk], v[blk], (((1,), (1,)), ((), ())), **exact)
            return (nt(xr, c_re) - nt(xi, c_im)) * same_group

        d0, d1 = feed(bbre, bbim), feed(abre, abim)
        w1_ref[j, 0:128, 0:512] = spread(abre).astype(bf16)
        w1_ref[j, 0:128, 512:1024] = spread(abim).astype(bf16)
        w1_ref[j, 0:128, 1024:1152] = d0.astype(bf16)
        w1_ref[j, 0:128, 1152:1280] = d1.astype(bf16)
        w1_ref[j, 128:256, 0:512] = spread(bbre).astype(bf16)
        w1_ref[j, 128:256, 512:1024] = spread(bbim).astype(bf16)
        w1_ref[j, 128:256, 1024:1152] = jnp.zeros((128, 128), bf16)
        w1_ref[j, 128:256, 1152:1280] = d0.astype(bf16)
        w2_ref[j, 0:512, 0:128] = spread(care).T.astype(bf16)
        w2_ref[j, 0:512, 128:256] = spread(c2re).T.astype(bf16)
        w2_ref[j, 512:1024, 0:128] = (-spread(caim)).T.astype(bf16)
        w2_ref[j, 512:1024, 128:256] = (-spread(c2im)).T.astype(bf16)


def _s5_prep(lambda_re, lambda_im, log_dt, b_re, b_im, c_re, c_im):
    g, p, h = b_re.shape
    assert (g, p, h) == (N_S5_GROUPS, S5_STATE, S5_GROUP)
    out = jax.ShapeDtypeStruct
    tile = jnp.tile(jnp.eye(p, dtype=f32), (1, 8))
    return pl.pallas_call(
        _s5_prep_kernel,
        out_shape=(out((2, g, p), f32), out((N_S5_BLOCKS, 256, 1280), bf16),
                   out((N_S5_BLOCKS, 1024, 256), bf16)),
        name="s5_prep",
    )(lambda_re, lambda_im, log_dt.reshape(g, 1),
      jnp.transpose(b_re, (0, 2, 1)), jnp.transpose(b_im, (0, 2, 1)), c_re, c_im, tile)


def _mixer_kernel(*refs, batch, steps, start_pos, n_cast):
    x_ref, h0re_ref, h0im_ref, hist_ref = refs[:4]
    (gmix_ref, win_ref, a2re_ref, a2im_ref, w1_ref, w2_ref, dskip_ref, wglu_ref, wpool_ref,
     pscale_ref, gs5_ref, gpool_ref, wout_ref) = refs[4:4 + N_MIXER_W]
    cast_in = refs[4 + N_MIXER_W:4 + N_MIXER_W + n_cast]
    outs = refs[4 + N_MIXER_W + n_cast:]
    x1_ref, hre_ref, him_ref, histout_ref = outs[:4]
    cast_out = outs[4:4 + n_cast]
    v_scr, h_scr, xp_scr, zs_scr, perm_scr = outs[4 + n_cast:]
    c = pl.program_id(0)

    for src, dst in zip(cast_in, cast_out):
        dst[...] = src[...].astype(bf16)

    rows = batch * steps
    half = rows // 2
    hist_rows = HIST_FRAMES * batch
    pitch = _perm_pitch(steps)
    n_slabs = D_MODEL // LANES

    def state_cols(j):
        lo = j * 2 * S5_BLOCK_STATES
        return slice(lo, lo + S5_BLOCK_STATES), slice(lo + S5_BLOCK_STATES, lo + 2 * S5_BLOCK_STATES)

    def block_cols(j):
        return slice(j * S5_BLOCK_STATES, (j + 1) * S5_BLOCK_STATES)

    @pl.when(c == 0)
    def _():
        for j in range(N_S5_BLOCKS):
            re_cols, im_cols = state_cols(j)
            h_scr[:, re_cols] = h0re_ref[:, block_cols(j)]
            h_scr[:, im_cols] = h0im_ref[:, block_cols(j)]
        xp_scr[0:hist_rows, :] = hist_ref[...]

    x = x_ref[...].reshape(rows, D_MODEL)
    z = _dot(_rmsnorm(x, gmix_ref[...]).astype(bf16), win_ref[...])

    for k in range(n_slabs):
        for b in range(batch):
            perm_scr[k, b * pitch:b * pitch + steps, :] = z[b * steps:(b + 1) * steps,
                                                            k * LANES:(k + 1) * LANES]
    for t in range(steps):
        q0 = (t // 2) * batch
        for k in range(n_slabs):
            v = perm_scr[k, pl.ds(t, batch, stride=pitch), :]
            if k * LANES < D_S5:
                zs_scr[t % 2, q0:q0 + batch, k * LANES:(k + 1) * LANES] = v
            else:
                xp_scr[hist_rows + t * batch:hist_rows + (t + 1) * batch,
                       k * LANES - D_S5:(k + 1) * LANES - D_S5] = v
    z_even, z_odd = zs_scr[0], zs_scr[1]

    feeds = []
    for j in range(N_S5_BLOCKS):
        ch = slice(j * 128, (j + 1) * 128)
        u = jnp.concatenate([z_even[:, ch], z_odd[:, ch]], axis=-1).astype(bf16)
        r = _dot(u, w1_ref[j])
        v_scr[:, j * 1024:(j + 1) * 1024] = r[:, :1024]
        feeds.append(r[:, 1024:])

    width = STATE_VREG_ELEMS // batch
    for k in range(N_S5_GROUPS * S5_STATE // width):
        j, o = divmod(k * width, S5_BLOCK_STATES)
        cre = j * 2 * S5_BLOCK_STATES + o
        cim = cre + S5_BLOCK_STATES
        ar = jnp.broadcast_to(a2re_ref[:, k * width:(k + 1) * width], (batch, width))
        ai = jnp.broadcast_to(a2im_ref[:, k * width:(k + 1) * width], (batch, width))

        hr, hi = h_scr[:, cre:cre + width], h_scr[:, cim:cim + width]
        for q in range(steps // 2):
            r0 = q * batch
            vr = v_scr[r0:r0 + batch, cre:cre + width]
            vi = v_scr[r0:r0 + batch, cim:cim + width]
            v_scr[r0:r0 + batch, cre:cre + width] = hr
            v_scr[r0:r0 + batch, cim:cim + width] = hi
            hr, hi = ar * hr - ai * hi + vr, ar * hi + ai * hr + vi
        h_scr[:, cre:cre + width] = hr
        h_scr[:, cim:cim + width] = hi

    y_even, y_odd = [], []
    for j in range(N_S5_BLOCKS):
        yy = _dot(v_scr[:, j * 1024:(j + 1) * 1024].astype(bf16), w2_ref[j]) + feeds[j]
        y_even.append(yy[:, :128])
        y_odd.append(yy[:, 128:])
    y = jnp.concatenate([jnp.concatenate(y_even, axis=-1), jnp.concatenate(y_odd, axis=-1)], axis=0)
    zs = jnp.concatenate([z_even, z_odd], axis=0)
    y = jax.nn.gelu(y + dskip_ref[...] * zs)
    y = y * jax.nn.sigmoid(_dot(y.astype(bf16), wglu_ref[...]))
    y_s5 = _rmsnorm(y, gs5_ref[...])

    t_idx = lax.shift_right_logical(lax.broadcasted_iota(jnp.int32, (rows, 1), 0),
                                    int(math.log2(batch)))
    pos1 = start_pos + c * steps + t_idx + 1
    outs = []
    for g, w in enumerate(POOL_WINDOWS):
        ext = xp_scr[:, g * POOL_GROUP:(g + 1) * POOL_GROUP]
        s = ext
        k = 1
        while k < w:
            s = s[k * batch:] + s[:-k * batch]
            k *= 2
        win = s[(HIST_FRAMES - (w - 1)) * batch:]
        cnt = jnp.minimum(pos1, w).astype(f32)
        pooled = win * (1.0 / cnt) - ext[hist_rows:]
        outs.append(_dot(pooled.astype(bf16), wpool_ref[g]))
    y_pool = _rmsnorm(jnp.concatenate(outs, axis=-1) * pscale_ref[...], gpool_ref[...])

    for t in range(steps):
        src = (t % 2) * half + (t // 2) * batch
        for k in range(n_slabs):
            if k * LANES < D_S5:
                v = y_s5[src:src + batch, k * LANES:(k + 1) * LANES]
            else:
                v = y_pool[t * batch:(t + 1) * batch, k * LANES - D_S5:(k + 1) * LANES - D_S5]
            perm_scr[k, pl.ds(t, batch, stride=pitch), :] = v
    mixed = jnp.concatenate(
        [jnp.concatenate([perm_scr[k, b * pitch:b * pitch + steps, :] for k in range(n_slabs)], axis=-1)
         for b in range(batch)], axis=0).astype(bf16)
    x1_ref[...] = (x + _dot(mixed, wout_ref[...])).reshape(batch, steps, D_MODEL)

    xp_scr[0:hist_rows, :] = xp_scr[rows:rows + hist_rows, :]

    @pl.when(c == pl.num_programs(0) - 1)
    def _():
        for j in range(N_S5_BLOCKS):
            re_cols, im_cols = state_cols(j)
            hre_ref[:, block_cols(j)] = h_scr[:, re_cols]
            him_ref[:, block_cols(j)] = h_scr[:, im_cols]
        histout_ref[...] = xp_scr[0:hist_rows, :]


N_MIXER_W = 13


def _const_spec(shape):
    return pl.BlockSpec(shape, lambda c: (0,) * len(shape), pipeline_mode=pl.Buffered(1))


def _perm_pitch(steps):
    tiles = steps // SUBLANES
    return SUBLANES * (tiles + 1 if tiles % 2 == 0 else tiles + 2)


def _mixer(x, h0re, h0im, hist, start_pos, wts, to_bf16=()):
    batch, length, _ = x.shape
    assert MIXER_ROWS % batch == 0 and len(wts) == N_MIXER_W
    steps = MIXER_ROWS // batch
    assert length % steps == 0 and steps % SUBLANES == 0
    assert steps >= HIST_FRAMES and steps % 2 == 0 and STATE_VREG_ELEMS % batch == 0
    hist_rows = HIST_FRAMES * batch
    n_steps = length // steps
    kern = functools.partial(_mixer_kernel, batch=batch, steps=steps, start_pos=start_pos,
                             n_cast=len(to_bf16))
    out = jax.ShapeDtypeStruct
    x_spec = pl.BlockSpec((batch, steps, D_MODEL), lambda c: (0, c, 0))
    cast_specs = []
    for w, axis in to_bf16:
        size = w.shape[axis] // n_steps
        assert w.ndim == 2 and size * n_steps == w.shape[axis] and size % (LANES if axis else 2 * SUBLANES) == 0
        cast_specs.append(pl.BlockSpec((w.shape[0], size), lambda c: (0, c)) if axis else
                          pl.BlockSpec((size, w.shape[1]), lambda c: (c, 0)))
    return pl.pallas_call(
        kern,
        grid=(n_steps,),
        in_specs=[x_spec, _const_spec(h0re.shape), _const_spec(h0im.shape), _const_spec(hist.shape)]
        + [_const_spec(w.shape) for w in wts] + cast_specs,
        out_specs=(x_spec,
                   pl.BlockSpec(h0re.shape, lambda c: (0, 0)),
                   pl.BlockSpec(h0im.shape, lambda c: (0, 0)),
                   pl.BlockSpec(hist.shape, lambda c: (0, 0)), *cast_specs),
        out_shape=(out(x.shape, f32), out(h0re.shape, f32), out(h0im.shape, f32), out(hist.shape, f32),
                   *(out(w.shape, bf16) for w, _ in to_bf16)),
        scratch_shapes=[pltpu.VMEM((MIXER_ROWS // 2, S5_COLS), f32),
                        pltpu.VMEM((batch, S5_COLS), f32),
                        pltpu.VMEM((hist_rows + MIXER_ROWS, D_POOL), f32),
                        pltpu.VMEM((2, MIXER_ROWS // 2, D_S5), f32),
                        pltpu.VMEM((D_MODEL // LANES, batch * _perm_pitch(steps), LANES), f32)],
        compiler_params=pltpu.CompilerParams(dimension_semantics=("arbitrary",),
                                             vmem_limit_bytes=VMEM_LIMIT_BYTES),
        name="mixer",
    )(x, h0re, h0im, hist, *wts, *(w for w, _ in to_bf16))


def _ffn_kernel(x_ref, p_ref, gmlp_ref, wup_ref, wdown_ref, gple_ref, wgate_ref, wproj_ref,
                gfinal_ref, o_ref):
    x = x_ref[...]
    xn = _rmsnorm(x, gmlp_ref[...]).astype(bf16)
    acc = x
    for f in range(D_FF // FFN_CHUNK):
        up = _dot(xn, wup_ref[:, f * FFN_CHUNK:(f + 1) * FFN_CHUNK])
        act = jnp.square(jnp.maximum(up, 0.0)).astype(bf16)
        acc = acc + _dot(act, wdown_ref[f * FFN_CHUNK:(f + 1) * FFN_CHUNK, :])
    gate = jax.nn.sigmoid(_dot(_rmsnorm(acc, gple_ref[...]).astype(bf16), wgate_ref[...]))
    x3 = acc + _dot(p_ref[...].astype(bf16), wproj_ref[...]) * gate
    o_ref[...] = _rmsnorm(x3, gfinal_ref[...])


def _ffn(x1, p, wts):
    n_rows = x1.shape[0]
    assert n_rows % FFN_ROWS == 0
    return pl.pallas_call(
        _ffn_kernel,
        grid=(n_rows // FFN_ROWS,),
        in_specs=[pl.BlockSpec((FFN_ROWS, D_MODEL), lambda i: (i, 0)),
                  pl.BlockSpec((FFN_ROWS, D_PLE), lambda i: (i, 0))] + [_const_spec(w.shape) for w in wts],
        out_specs=pl.BlockSpec((FFN_ROWS, D_MODEL), lambda i: (i, 0)),
        out_shape=jax.ShapeDtypeStruct((n_rows, D_MODEL), f32),
        compiler_params=pltpu.CompilerParams(dimension_semantics=("arbitrary",),
                                             vmem_limit_bytes=VMEM_LIMIT_BYTES),
        name="ffn",
    )(x1, p, *wts)


def _pack_hist(hist):
    b = hist.shape[0]
    h = jnp.pad(jnp.transpose(hist, (1, 0, 2)), ((1, 0), (0, 0), (0, 0)))
    return h.reshape(HIST_FRAMES * b, D_POOL)


def _unpack_hist(h, b):
    return jnp.transpose(h.reshape(HIST_FRAMES, b, D_POOL)[1:], (1, 0, 2))


def kernel(x_prompt, x_sample, state_s5_re, state_s5_im, state_pool, p_prompt, p_sample, g_mix_norm, w_in, lambda_re, lambda_im, log_dt, b_re, b_im, c_re, c_im, d_skip, w_glu, w_pool, pool_scale, g_s5_out, g_pool_out, w_out, g_mlp_norm, w_up, w_down, g_ple_norm, w_ple_gate, w_ple_proj, g_final):
    assert x_prompt.shape[2] == D_MODEL and w_in.shape[0] == 1
    bsz_p, bsz_s = x_prompt.shape[0], x_sample.shape[0]
    row = lambda v: v.reshape(1, -1)

    a2, w1, w2 = _s5_prep(lambda_re[0], lambda_im[0], log_dt[0], b_re[0], b_im[0], c_re[0], c_im[0])

    mixer_wts = (row(g_mix_norm[0]), w_in[0].astype(bf16), row(a2[0]), row(a2[1]), w1, w2,
                 row(d_skip[0]), w_glu[0].astype(bf16), w_pool[0].astype(bf16), row(pool_scale[0]),
                 row(g_s5_out[0]), row(g_pool_out[0]), w_out[0].astype(bf16))
    def mix(x, h0_re, h0_im, hist, start_pos, to_bf16=()):
        b = x.shape[0]
        flat = lambda h: h.reshape(b, N_S5_GROUPS * S5_STATE)
        x1, re, im, hist_new, *cast = _mixer(x, flat(h0_re), flat(h0_im), _pack_hist(hist), start_pos,
                                             mixer_wts, to_bf16)
        state = lambda h: h.reshape(1, b, N_S5_GROUPS, S5_STATE)
        return x1, state(re), state(im), _unpack_hist(hist_new, b)[None], cast

    def ffn(x1, p, ffn_wts):
        y = _ffn(x1.reshape(-1, D_MODEL), p.reshape(-1, D_PLE), ffn_wts)
        return y.reshape(x1.shape)

    zeros_state = jnp.zeros((bsz_p, N_S5_GROUPS, S5_STATE), f32)
    x1_p, re_p, im_p, pool_p, (wup, wdown, wgate) = mix(
        x_prompt, zeros_state, zeros_state, jnp.zeros((bsz_p, POOL_HIST, D_POOL), f32), 0,
        ((w_up[0], 1), (w_down[0], 0), (w_ple_gate[0], 0)))
    ffn_wts = (row(g_mlp_norm[0]), wup, wdown, row(g_ple_norm[0]), wgate, w_ple_proj[0].astype(bf16),
               row(g_final))
    y_p = ffn(x1_p, p_prompt[0], ffn_wts)
    x1_s, re_s, im_s, pool_s, _ = mix(x_sample, state_s5_re[0], state_s5_im[0], state_pool[0], PAST_LEN)
    y_s = ffn(x1_s, p_sample[0], ffn_wts)
    return (y_p, y_s, re_p, im_p, pool_p, re_s, im_s, pool_s)
```

```python
import functools
import math

import jax
import jax.numpy as jnp
from jax import lax
from jax.experimental import pallas as pl
from jax.experimental.pallas import tpu as pltpu

D_MODEL = 1024
D_S5 = 512
D_POOL = 512
S5_GROUP = 16
N_S5_GROUPS = 32
S5_STATE = 64
POOL_WINDOWS = (2, 4, 8, 16)
POOL_GROUP = 128
POOL_HIST = 15
HIST_FRAMES = POOL_HIST + 1
D_FF = 4096
D_PLE = 256
EPS = 1e-6
PAST_LEN = 1024

N_S5_BLOCKS = 4
S5_BLOCK_STATES = 512
S5_COLS = 2 * N_S5_GROUPS * S5_STATE
SUBLANES, LANES = 8, 128
STATE_VREG_ELEMS = 4 * SUBLANES * LANES

MIXER_ROWS = 512
MIXER_SUBCHUNKS = 2
FFN_ROWS = 1024
FFN_CHUNK = 1024
VMEM_LIMIT_BYTES = 60 * 1024 * 1024

f32 = jnp.float32
bf16 = jnp.bfloat16


def _rms(x):
    return x * lax.rsqrt(jnp.mean(x * x, axis=-1, keepdims=True) + EPS)


def _rmsnorm(x, g):
    return _rms(x) * g


def _dot(a, b):
    return jnp.dot(a, b, preferred_element_type=f32)


def _s5_prep_kernel(lre_ref, lim_ref, logdt_ref, bre_ref, bim_ref, cre_ref, cim_ref, tile_ref,
                    a2_ref, w1_ref, w2_ref):
    lre = lre_ref[...]
    lim = lim_ref[...]
    dt = jnp.exp(logdt_ref[...])
    mag = jnp.exp(lre * dt)
    ang = lim * dt
    are = mag * jnp.cos(ang)
    aim = mag * jnp.sin(ang)
    a2re = are * are - aim * aim
    a2im = 2.0 * are * aim
    a2_ref[0] = a2re
    a2_ref[1] = a2im
    nr = are - 1.0
    ni = aim
    den = lre * lre + lim * lim
    kre = ((nr * lre + ni * lim) / den)[:, None, :]
    kim = ((ni * lre - nr * lim) / den)[:, None, :]

    def cmul(xr, xi, yr, yi):
        return xr * yr - xi * yi, xr * yi + xi * yr

    flat = lambda v: v.reshape(N_S5_GROUPS * S5_GROUP, S5_STATE)
    ar3, ai3 = are[:, None, :], aim[:, None, :]
    bbre, bbim = cmul(kre, kim, bre_ref[...], bim_ref[...])
    abre, abim = cmul(ar3, ai3, bbre, bbim)
    c_re, c_im = cre_ref[...], cim_ref[...]
    care, caim = cmul(ar3, ai3, c_re, c_im)
    c2re, c2im = cmul(a2re[:, None, :], a2im[:, None, :], c_re, c_im)
    bbre, bbim, abre, abim, care, caim, c2re, c2im, c_re, c_im = map(
        flat, (bbre, bbim, abre, abim, care, caim, c2re, c2im, c_re, c_im))

    exact = dict(precision=lax.Precision.HIGHEST, preferred_element_type=f32)
    row_group = lax.shift_right_logical(lax.broadcasted_iota(jnp.int32, (128, 512), 0), 4)
    col_group = lax.shift_right_logical(lax.broadcasted_iota(jnp.int32, (128, 512), 1), 6)
    own_states = (row_group == col_group).astype(f32)
    same_group = (row_group[:, :128] ==
                  lax.shift_right_logical(lax.broadcasted_iota(jnp.int32, (128, 128), 1), 4)).astype(f32)

    for j in range(N_S5_BLOCKS):
        blk = slice(j * 128, (j + 1) * 128)

        def spread(v):
            return jnp.dot(v[blk], tile_ref[...], **exact) * own_states

        def feed(xr, xi):
            nt = lambda u, v: lax.dot_general(u[blk], v[blk], (((1,), (1,)), ((), ())), **exact)
            return (nt(xr, c_re) - nt(xi, c_im)) * same_group

        d0, d1 = feed(bbre, bbim), feed(abre, abim)
        w1_ref[j, 0:128, 0:512] = spread(abre).astype(bf16)
        w1_ref[j, 0:128, 512:1024] = spread(abim).astype(bf16)
        w1_ref[j, 0:128, 1024:1152] = d0.astype(bf16)
        w1_ref[j, 0:128, 1152:1280] = d1.astype(bf16)
        w1_ref[j, 128:256, 0:512] = spread(bbre).astype(bf16)
        w1_ref[j, 128:256, 512:1024] = spread(bbim).astype(bf16)
        w1_ref[j, 128:256, 1024:1152] = jnp.zeros((128, 128), bf16)
        w1_ref[j, 128:256, 1152:1280] = d0.astype(bf16)
        w2_ref[j, 0:512, 0:128] = spread(care).T.astype(bf16)
        w2_ref[j, 0:512, 128:256] = spread(c2re).T.astype(bf16)
        w2_ref[j, 512:1024, 0:128] = (-spread(caim)).T.astype(bf16)
        w2_ref[j, 512:1024, 128:256] = (-spread(c2im)).T.astype(bf16)


def _s5_prep(lambda_re, lambda_im, log_dt, b_re, b_im, c_re, c_im):
    g, p, h = b_re.shape
    assert (g, p, h) == (N_S5_GROUPS, S5_STATE, S5_GROUP)
    out = jax.ShapeDtypeStruct
    tile = jnp.tile(jnp.eye(p, dtype=f32), (1, 8))
    return pl.pallas_call(
        _s5_prep_kernel,
        out_shape=(out((2, g, p), f32), out((N_S5_BLOCKS, 256, 1280), bf16),
                   out((N_S5_BLOCKS, 1024, 256), bf16)),
        name="s5_prep",
    )(lambda_re, lambda_im, log_dt.reshape(g, 1),
      jnp.transpose(b_re, (0, 2, 1)), jnp.transpose(b_im, (0, 2, 1)), c_re, c_im, tile)


def _mixer_phases(s, c, x_ref, wts, x1_ref, v_scr, h_scr, xp_scr, zs_scr, perm_scr, *,
                  batch, steps, n_sub, start_pos):
    (win_ref, a2re_ref, a2im_ref, w1_ref, w2_ref, dskip_ref, wglu_ref, wpool_ref, pscale_ref,
     wout_ref) = wts
    rows = batch * steps
    half = rows // 2
    hist_rows = HIST_FRAMES * batch
    pitch = _perm_pitch(steps)
    n_slabs = D_MODEL // LANES
    prev = (s - 1) % n_sub
    frames = slice(s * steps, (s + 1) * steps)

    x = x_ref[:, frames, :].reshape(rows, D_MODEL)
    z = _dot(_rms(x).astype(bf16), win_ref[...])
    for k in range(n_slabs):
        for b in range(batch):
            perm_scr[s, k, b * pitch:b * pitch + steps, :] = z[b * steps:(b + 1) * steps,
                                                               k * LANES:(k + 1) * LANES]
    yield

    xp_scr[s, 0:hist_rows, :] = xp_scr[prev, rows:rows + hist_rows, :]
    for t in range(steps):
        q0 = (t // 2) * batch
        for k in range(n_slabs):
            v = perm_scr[s, k, pl.ds(t, batch, stride=pitch), :]
            if k * LANES < D_S5:
                zs_scr[s, t % 2, q0:q0 + batch, k * LANES:(k + 1) * LANES] = v
            else:
                xp_scr[s, hist_rows + t * batch:hist_rows + (t + 1) * batch,
                       k * LANES - D_S5:(k + 1) * LANES - D_S5] = v
    yield

    z_even, z_odd = zs_scr[s, 0], zs_scr[s, 1]
    feeds = []
    for j in range(N_S5_BLOCKS):
        ch = slice(j * 128, (j + 1) * 128)
        u = jnp.concatenate([z_even[:, ch], z_odd[:, ch]], axis=-1).astype(bf16)
        r = _dot(u, w1_ref[j])
        v_scr[s, :, j * 1024:(j + 1) * 1024] = r[:, :1024]
        feeds.append(r[:, 1024:])
    yield

    width = STATE_VREG_ELEMS // batch
    for k in range(N_S5_GROUPS * S5_STATE // width):
        j, o = divmod(k * width, S5_BLOCK_STATES)
        cre = j * 2 * S5_BLOCK_STATES + o
        cim = cre + S5_BLOCK_STATES
        ar = jnp.broadcast_to(a2re_ref[:, k * width:(k + 1) * width], (batch, width))
        ai = jnp.broadcast_to(a2im_ref[:, k * width:(k + 1) * width], (batch, width))

        hr, hi = h_scr[:, cre:cre + width], h_scr[:, cim:cim + width]
        for q in range(steps // 2):
            r0 = q * batch
            vr = v_scr[s, r0:r0 + batch, cre:cre + width]
            vi = v_scr[s, r0:r0 + batch, cim:cim + width]
            v_scr[s, r0:r0 + batch, cre:cre + width] = hr
            v_scr[s, r0:r0 + batch, cim:cim + width] = hi
            hr, hi = ar * hr - ai * hi + vr, ar * hi + ai * hr + vi
        h_scr[:, cre:cre + width] = hr
        h_scr[:, cim:cim + width] = hi
    yield

    y_even, y_odd = [], []
    for j in range(N_S5_BLOCKS):
        yy = _dot(v_scr[s, :, j * 1024:(j + 1) * 1024].astype(bf16), w2_ref[j]) + feeds[j]
        y_even.append(yy[:, :128])
        y_odd.append(yy[:, 128:])
    y = jnp.concatenate([jnp.concatenate(y_even, axis=-1), jnp.concatenate(y_odd, axis=-1)], axis=0)
    zs = jnp.concatenate([z_even, z_odd], axis=0)
    y = jax.nn.gelu(y + dskip_ref[...] * zs)
    y = y * jax.nn.sigmoid(_dot(y.astype(bf16), wglu_ref[...]))
    y_s5 = _rms(y)
    yield

    t_idx = lax.shift_right_logical(lax.broadcasted_iota(jnp.int32, (rows, 1), 0),
                                    int(math.log2(batch)))
    pos1 = start_pos + (c * n_sub + s) * steps + t_idx + 1
    outs = []
    for g, w in enumerate(POOL_WINDOWS):
        ext = xp_scr[s, :, g * POOL_GROUP:(g + 1) * POOL_GROUP]
        acc = ext
        k = 1
        while k < w:
            acc = acc[k * batch:] + acc[:-k * batch]
            k *= 2
        win = acc[(HIST_FRAMES - (w - 1)) * batch:]
        cnt = jnp.minimum(pos1, w).astype(f32)
        pooled = win * (1.0 / cnt) - ext[hist_rows:]
        outs.append(_dot(pooled.astype(bf16), wpool_ref[g]))
    y_pool = _rms(jnp.concatenate(outs, axis=-1) * pscale_ref[...])
    yield

    for t in range(steps):
        src = (t % 2) * half + (t // 2) * batch
        for k in range(n_slabs):
            if k * LANES < D_S5:
                v = y_s5[src:src + batch, k * LANES:(k + 1) * LANES]
            else:
                v = y_pool[t * batch:(t + 1) * batch, k * LANES - D_S5:(k + 1) * LANES - D_S5]
            perm_scr[s, k, pl.ds(t, batch, stride=pitch), :] = v
    mixed = jnp.concatenate(
        [jnp.concatenate([perm_scr[s, k, b * pitch:b * pitch + steps, :] for k in range(n_slabs)],
                         axis=-1) for b in range(batch)], axis=0).astype(bf16)
    x1_ref[:, frames, :] = (x + _dot(mixed, wout_ref[...])).reshape(batch, steps, D_MODEL)


SUBCHUNK_ORDER = {1: "", 2: "ABAABBAAABBBAB"}


def _mixer_kernel(*refs, batch, steps, n_sub, start_pos, n_cast):
    x_ref, h0re_ref, h0im_ref, hist_ref = refs[:4]
    wts = refs[4:4 + N_MIXER_W]
    cast_in = refs[4 + N_MIXER_W:4 + N_MIXER_W + n_cast]
    outs = refs[4 + N_MIXER_W + n_cast:]
    x1_ref, hre_ref, him_ref, histout_ref = outs[:4]
    cast_out = outs[4:4 + n_cast]
    v_scr, h_scr, xp_scr, zs_scr, perm_scr = outs[4 + n_cast:]
    c = pl.program_id(0)
    rows = batch * steps
    hist_rows = HIST_FRAMES * batch
    last = n_sub - 1

    for src, dst in zip(cast_in, cast_out):
        dst[...] = src[...].astype(bf16)

    def state_cols(j):
        lo = j * 2 * S5_BLOCK_STATES
        return slice(lo, lo + S5_BLOCK_STATES), slice(lo + S5_BLOCK_STATES, lo + 2 * S5_BLOCK_STATES)

    def block_cols(j):
        return slice(j * S5_BLOCK_STATES, (j + 1) * S5_BLOCK_STATES)

    @pl.when(c == 0)
    def _():
        for j in range(N_S5_BLOCKS):
            re_cols, im_cols = state_cols(j)
            h_scr[:, re_cols] = h0re_ref[:, block_cols(j)]
            h_scr[:, im_cols] = h0im_ref[:, block_cols(j)]
        xp_scr[last, rows:rows + hist_rows, :] = hist_ref[...]

    gens = [_mixer_phases(s, c, x_ref, wts, x1_ref, v_scr, h_scr, xp_scr, zs_scr, perm_scr,
                          batch=batch, steps=steps, n_sub=n_sub, start_pos=start_pos)
            for s in range(n_sub)]
    for tag in SUBCHUNK_ORDER[n_sub]:
        next(gens["AB".index(tag)], None)
    for gen in gens:
        for _ in gen:
            pass

    @pl.when(c == pl.num_programs(0) - 1)
    def _():
        for j in range(N_S5_BLOCKS):
            re_cols, im_cols = state_cols(j)
            hre_ref[:, block_cols(j)] = h_scr[:, re_cols]
            him_ref[:, block_cols(j)] = h_scr[:, im_cols]
        histout_ref[...] = xp_scr[last, rows:rows + hist_rows, :]


N_MIXER_W = 10


def _const_spec(shape):
    return pl.BlockSpec(shape, lambda c: (0,) * len(shape), pipeline_mode=pl.Buffered(1))


def _perm_pitch(steps):
    tiles = steps // SUBLANES
    return SUBLANES * (tiles + 1 if tiles % 2 == 0 else tiles + 2)


def _mixer(x, h0re, h0im, hist, start_pos, wts, to_bf16=()):
    batch, length, _ = x.shape
    assert MIXER_ROWS % batch == 0 and len(wts) == N_MIXER_W
    steps = MIXER_ROWS // batch
    n_sub = MIXER_SUBCHUNKS
    assert length % (steps * n_sub) == 0 and steps % SUBLANES == 0
    assert steps >= HIST_FRAMES and steps % 2 == 0 and STATE_VREG_ELEMS % batch == 0
    hist_rows = HIST_FRAMES * batch
    n_steps = length // (steps * n_sub)
    kern = functools.partial(_mixer_kernel, batch=batch, steps=steps, n_sub=n_sub, start_pos=start_pos,
                             n_cast=len(to_bf16))
    out = jax.ShapeDtypeStruct
    x_spec = pl.BlockSpec((batch, steps * n_sub, D_MODEL), lambda c: (0, c, 0))
    cast_specs = []
    for w, axis in to_bf16:
        size = w.shape[axis] // n_steps
        assert w.ndim == 2 and size * n_steps == w.shape[axis] and size % (LANES if axis else 2 * SUBLANES) == 0
        cast_specs.append(pl.BlockSpec((w.shape[0], size), lambda c: (0, c)) if axis else
                          pl.BlockSpec((size, w.shape[1]), lambda c: (c, 0)))
    return pl.pallas_call(
        kern,
        grid=(n_steps,),
        in_specs=[x_spec, _const_spec(h0re.shape), _const_spec(h0im.shape), _const_spec(hist.shape)]
        + [_const_spec(w.shape) for w in wts] + cast_specs,
        out_specs=(x_spec,
                   pl.BlockSpec(h0re.shape, lambda c: (0, 0)),
                   pl.BlockSpec(h0im.shape, lambda c: (0, 0)),
                   pl.BlockSpec(hist.shape, lambda c: (0, 0)), *cast_specs),
        out_shape=(out(x.shape, f32), out(h0re.shape, f32), out(h0im.shape, f32), out(hist.shape, f32),
                   *(out(w.shape, bf16) for w, _ in to_bf16)),
        scratch_shapes=[pltpu.VMEM((n_sub, MIXER_ROWS // 2, S5_COLS), f32),
                        pltpu.VMEM((batch, S5_COLS), f32),
                        pltpu.VMEM((n_sub, hist_rows + MIXER_ROWS, D_POOL), f32),
                        pltpu.VMEM((n_sub, 2, MIXER_ROWS // 2, D_S5), f32),
                        pltpu.VMEM((n_sub, D_MODEL // LANES, batch * _perm_pitch(steps), LANES), f32)],
        compiler_params=pltpu.CompilerParams(dimension_semantics=("arbitrary",),
                                             vmem_limit_bytes=VMEM_LIMIT_BYTES),
        name="mixer",
    )(x, h0re, h0im, hist, *wts, *(w for w, _ in to_bf16))


def _ffn_kernel(x_ref, p_ref, gmlp_ref, wup_ref, wdown_ref, gple_ref, wgate_ref, wproj_ref,
                gfinal_ref, o_ref):
    x = x_ref[...]
    xn = _rmsnorm(x, gmlp_ref[...]).astype(bf16)
    acc = x
    for f in range(D_FF // FFN_CHUNK):
        up = _dot(xn, wup_ref[:, f * FFN_CHUNK:(f + 1) * FFN_CHUNK])
        act = jnp.square(jnp.maximum(up, 0.0)).astype(bf16)
        acc = acc + _dot(act, wdown_ref[f * FFN_CHUNK:(f + 1) * FFN_CHUNK, :])
    gate = jax.nn.sigmoid(_dot(_rmsnorm(acc, gple_ref[...]).astype(bf16), wgate_ref[...]))
    x3 = acc + _dot(p_ref[...].astype(bf16), wproj_ref[...]) * gate
    o_ref[...] = _rmsnorm(x3, gfinal_ref[...])


def _ffn(x1, p, wts):
    n_rows = x1.shape[0]
    assert n_rows % FFN_ROWS == 0
    return pl.pallas_call(
        _ffn_kernel,
        grid=(n_rows // FFN_ROWS,),
        in_specs=[pl.BlockSpec((FFN_ROWS, D_MODEL), lambda i: (i, 0)),
                  pl.BlockSpec((FFN_ROWS, D_PLE), lambda i: (i, 0))] + [_const_spec(w.shape) for w in wts],
        out_specs=pl.BlockSpec((FFN_ROWS, D_MODEL), lambda i: (i, 0)),
        out_shape=jax.ShapeDtypeStruct((n_rows, D_MODEL), f32),
        compiler_params=pltpu.CompilerParams(dimension_semantics=("arbitrary",),
                                             vmem_limit_bytes=VMEM_LIMIT_BYTES),
        name="ffn",
    )(x1, p, *wts)


def _pack_hist(hist):
    b = hist.shape[0]
    h = jnp.pad(jnp.transpose(hist, (1, 0, 2)), ((1, 0), (0, 0), (0, 0)))
    return h.reshape(HIST_FRAMES * b, D_POOL)


def _unpack_hist(h, b):
    return jnp.transpose(h.reshape(HIST_FRAMES, b, D_POOL)[1:], (1, 0, 2))


def kernel(x_prompt, x_sample, state_s5_re, state_s5_im, state_pool, p_prompt, p_sample, g_mix_norm, w_in, lambda_re, lambda_im, log_dt, b_re, b_im, c_re, c_im, d_skip, w_glu, w_pool, pool_scale, g_s5_out, g_pool_out, w_out, g_mlp_norm, w_up, w_down, g_ple_norm, w_ple_gate, w_ple_proj, g_final):
    assert x_prompt.shape[2] == D_MODEL and w_in.shape[0] == 1
    bsz_p, bsz_s = x_prompt.shape[0], x_sample.shape[0]
    row = lambda v: v.reshape(1, -1)

    a2, w1, w2 = _s5_prep(lambda_re[0], lambda_im[0], log_dt[0], b_re[0], b_im[0], c_re[0], c_im[0])

    scale_rows = lambda w, g: (w * g[:, None]).astype(bf16)
    g_mixed = jnp.concatenate([g_s5_out[0], g_pool_out[0]])
    mixer_wts = (scale_rows(w_in[0], g_mix_norm[0]), row(a2[0]), row(a2[1]), w1, w2,
                 row(d_skip[0]), w_glu[0].astype(bf16), w_pool[0].astype(bf16), row(pool_scale[0]),
                 scale_rows(w_out[0], g_mixed))

    def mix(x, h0_re, h0_im, hist, start_pos, to_bf16=()):
        b = x.shape[0]
        flat = lambda h: h.reshape(b, N_S5_GROUPS * S5_STATE)
        x1, re, im, hist_new, *cast = _mixer(x, flat(h0_re), flat(h0_im), _pack_hist(hist), start_pos,
                                             mixer_wts, to_bf16)
        state = lambda h: h.reshape(1, b, N_S5_GROUPS, S5_STATE)
        return x1, state(re), state(im), _unpack_hist(hist_new, b)[None], cast

    def ffn(x1, p, ffn_wts):
        y = _ffn(x1.reshape(-1, D_MODEL), p.reshape(-1, D_PLE), ffn_wts)
        return y.reshape(x1.shape)

    zeros_state = jnp.zeros((bsz_p, N_S5_GROUPS, S5_STATE), f32)
    x1_p, re_p, im_p, pool_p, (wup, wdown, wgate) = mix(
        x_prompt, zeros_state, zeros_state, jnp.zeros((bsz_p, POOL_HIST, D_POOL), f32), 0,
        ((w_up[0], 1), (w_down[0], 0), (w_ple_gate[0], 0)))
    ffn_wts = (row(g_mlp_norm[0]), wup, wdown, row(g_ple_norm[0]), wgate, w_ple_proj[0].astype(bf16),
               row(g_final))
    y_p = ffn(x1_p, p_prompt[0], ffn_wts)
    x1_s, re_s, im_s, pool_s, _ = mix(x_sample, state_s5_re[0], state_s5_im[0], state_pool[0], PAST_LEN)
    y_s = ffn(x1_s, p_sample[0], ffn_wts)
    return (y_p, y_s, re_p, im_p, pool_p, re_s, im_s, pool_s)
```

```python
import functools
import math

import jax
import jax.numpy as jnp
from jax import lax
from jax.experimental import pallas as pl
from jax.experimental.pallas import tpu as pltpu

D_MODEL = 1024
D_S5 = 512
D_POOL = 512
S5_GROUP = 16
N_S5_GROUPS = 32
S5_STATE = 64
POOL_WINDOWS = (2, 4, 8, 16)
POOL_GROUP = 128
POOL_HIST = 15
HIST_FRAMES = POOL_HIST + 1
D_FF = 4096
D_PLE = 256
EPS = 1e-6
PAST_LEN = 1024

N_S5_BLOCKS = 4
S5_BLOCK_STATES = 512
S5_COLS = 2 * N_S5_GROUPS * S5_STATE
SUBLANES, LANES = 8, 128
STATE_VREG_ELEMS = 4 * SUBLANES * LANES

MIXER_ROWS = 512
MIXER_SUBCHUNKS = 2
FFN_ROWS = 1024
FFN_CHUNK = 1024
FFN_TAIL_ROWS = 256
VMEM_LIMIT_BYTES = 60 * 1024 * 1024

f32 = jnp.float32
bf16 = jnp.bfloat16


def _rms(x):
    return x * lax.rsqrt(jnp.mean(x * x, axis=-1, keepdims=True) + EPS)


def _rmsnorm(x, g):
    return _rms(x) * g


def _dot(a, b):
    return jnp.dot(a, b, preferred_element_type=f32)


def _s5_prep_kernel(lre_ref, lim_ref, logdt_ref, bre_ref, bim_ref, cre_ref, cim_ref,
                    a2_ref, w1_ref, w2_ref):
    lre = lre_ref[...]
    lim = lim_ref[...]
    dt = jnp.exp(logdt_ref[...])
    mag = jnp.exp(lre * dt)
    ang = lim * dt
    are = mag * jnp.cos(ang)
    aim = mag * jnp.sin(ang)
    a2re = are * are - aim * aim
    a2im = 2.0 * are * aim
    a2_ref[0] = a2re
    a2_ref[1] = a2im
    nr = are - 1.0
    ni = aim
    den = lre * lre + lim * lim
    kre = ((nr * lre + ni * lim) / den)[:, None, :]
    kim = ((ni * lre - nr * lim) / den)[:, None, :]

    def cmul(xr, xi, yr, yi):
        return xr * yr - xi * yi, xr * yi + xi * yr

    flat = lambda v: v.reshape(N_S5_GROUPS * S5_GROUP, S5_STATE)
    ar3, ai3 = are[:, None, :], aim[:, None, :]
    bbre, bbim = cmul(kre, kim, bre_ref[...], bim_ref[...])
    abre, abim = cmul(ar3, ai3, bbre, bbim)
    c_re, c_im = cre_ref[...], cim_ref[...]
    care, caim = cmul(ar3, ai3, c_re, c_im)
    c2re, c2im = cmul(a2re[:, None, :], a2im[:, None, :], c_re, c_im)
    bbre, bbim, abre, abim, care, caim, c2re, c2im, c_re, c_im = map(
        flat, (bbre, bbim, abre, abim, care, caim, c2re, c2im, c_re, c_im))

    exact = dict(precision=lax.Precision.HIGHEST, preferred_element_type=f32)
    row_group = lax.shift_right_logical(lax.broadcasted_iota(jnp.int32, (128, 512), 0), 4)
    col_group = lax.shift_right_logical(lax.broadcasted_iota(jnp.int32, (128, 512), 1), 6)
    own_states = (row_group == col_group).astype(f32)
    same_group = (row_group[:, :128] ==
                  lax.shift_right_logical(lax.broadcasted_iota(jnp.int32, (128, 128), 1), 4)).astype(f32)

    for j in range(N_S5_BLOCKS):
        blk = slice(j * 128, (j + 1) * 128)

        def spread(v):
            return jnp.tile(v[blk], (1, 8)) * own_states

        def feed(xr, xi):
            nt = lambda u, v: lax.dot_general(u[blk], v[blk], (((1,), (1,)), ((), ())), **exact)
            return (nt(xr, c_re) - nt(xi, c_im)) * same_group

        d0, d1 = feed(bbre, bbim), feed(abre, abim)
        w1_ref[j, 0:128, 0:512] = spread(abre).astype(bf16)
        w1_ref[j, 0:128, 512:1024] = spread(abim).astype(bf16)
        w1_ref[j, 0:128, 1024:1152] = d0.astype(bf16)
        w1_ref[j, 0:128, 1152:1280] = d1.astype(bf16)
        w1_ref[j, 128:256, 0:512] = spread(bbre).astype(bf16)
        w1_ref[j, 128:256, 512:1024] = spread(bbim).astype(bf16)
        w1_ref[j, 128:256, 1024:1152] = jnp.zeros((128, 128), bf16)
        w1_ref[j, 128:256, 1152:1280] = d0.astype(bf16)
        w2_ref[j, 0:512, 0:128] = spread(care).T.astype(bf16)
        w2_ref[j, 0:512, 128:256] = spread(c2re).T.astype(bf16)
        w2_ref[j, 512:1024, 0:128] = (-spread(caim)).T.astype(bf16)
        w2_ref[j, 512:1024, 128:256] = (-spread(c2im)).T.astype(bf16)


def _s5_prep(lambda_re, lambda_im, log_dt, b_re, b_im, c_re, c_im):
    g, p, h = b_re.shape
    assert (g, p, h) == (N_S5_GROUPS, S5_STATE, S5_GROUP)
    out = jax.ShapeDtypeStruct
    return pl.pallas_call(
        _s5_prep_kernel,
        out_shape=(out((2, g, p), f32), out((N_S5_BLOCKS, 256, 1280), bf16),
                   out((N_S5_BLOCKS, 1024, 256), bf16)),
        name="s5_prep",
    )(lambda_re, lambda_im, log_dt.reshape(g, 1),
      jnp.transpose(b_re, (0, 2, 1)), jnp.transpose(b_im, (0, 2, 1)), c_re, c_im)


def _mixer_phases(s, c, x_ref, wts, x1_ref, v_scr, h_scr, xp_scr, zs_scr, perm_scr, *,
                  batch, steps, n_sub, start_pos):
    (win_ref, a2re_ref, a2im_ref, w1_ref, w2_ref, dskip_ref, wglu_ref, wpool_ref, pscale_ref,
     wout_ref) = wts
    rows = batch * steps
    half = rows // 2
    hist_rows = HIST_FRAMES * batch
    pitch = _perm_pitch(steps)
    n_slabs = D_MODEL // LANES
    prev = (s - 1) % n_sub
    frames = slice(s * steps, (s + 1) * steps)

    x = x_ref[:, frames, :].reshape(rows, D_MODEL)
    z = _dot(_rms(x).astype(bf16), win_ref[...])
    for k in range(n_slabs):
        for b in range(batch):
            perm_scr[s, k, b * pitch:b * pitch + steps, :] = z[b * steps:(b + 1) * steps,
                                                               k * LANES:(k + 1) * LANES]
    yield

    xp_scr[s, 0:hist_rows, :] = xp_scr[prev, rows:rows + hist_rows, :]
    for t in range(steps):
        q0 = (t // 2) * batch
        for k in range(n_slabs):
            v = perm_scr[s, k, pl.ds(t, batch, stride=pitch), :]
            if k * LANES < D_S5:
                zs_scr[s, t % 2, q0:q0 + batch, k * LANES:(k + 1) * LANES] = v
            else:
                xp_scr[s, hist_rows + t * batch:hist_rows + (t + 1) * batch,
                       k * LANES - D_S5:(k + 1) * LANES - D_S5] = v
    yield

    t_idx = lax.shift_right_logical(lax.broadcasted_iota(jnp.int32, (rows, 1), 0),
                                    int(math.log2(batch)))
    pos1 = start_pos + (c * n_sub + s) * steps + t_idx + 1
    outs = []
    for g, w in enumerate(POOL_WINDOWS):
        ext = xp_scr[s, :, g * POOL_GROUP:(g + 1) * POOL_GROUP]
        acc = ext
        k = 1
        while k < w:
            acc = acc[k * batch:] + acc[:-k * batch]
            k *= 2
        win = acc[(HIST_FRAMES - (w - 1)) * batch:]
        cnt = jnp.minimum(pos1, w).astype(f32)
        pooled = win * (1.0 / cnt) - ext[hist_rows:]
        outs.append(_dot(pooled.astype(bf16), wpool_ref[g]))
    y_pool = _rms(jnp.concatenate(outs, axis=-1) * pscale_ref[...])
    yield

    z_even, z_odd = zs_scr[s, 0], zs_scr[s, 1]
    feeds = []
    for j in range(N_S5_BLOCKS):
        ch = slice(j * 128, (j + 1) * 128)
        u = jnp.concatenate([z_even[:, ch], z_odd[:, ch]], axis=-1).astype(bf16)
        r = _dot(u, w1_ref[j])
        v_scr[s, :, j * 1024:(j + 1) * 1024] = r[:, :1024]
        feeds.append(r[:, 1024:])
    yield

    width = STATE_VREG_ELEMS // batch
    for k in range(N_S5_GROUPS * S5_STATE // width):
        j, o = divmod(k * width, S5_BLOCK_STATES)
        cre = j * 2 * S5_BLOCK_STATES + o
        cim = cre + S5_BLOCK_STATES
        ar = jnp.broadcast_to(a2re_ref[:, k * width:(k + 1) * width], (batch, width))
        ai = jnp.broadcast_to(a2im_ref[:, k * width:(k + 1) * width], (batch, width))

        hr, hi = h_scr[:, cre:cre + width], h_scr[:, cim:cim + width]
        for q in range(steps // 2):
            r0 = q * batch
            vr = v_scr[s, r0:r0 + batch, cre:cre + width]
            vi = v_scr[s, r0:r0 + batch, cim:cim + width]
            v_scr[s, r0:r0 + batch, cre:cre + width] = hr
            v_scr[s, r0:r0 + batch, cim:cim + width] = hi
            hr, hi = ar * hr - ai * hi + vr, ar * hi + ai * hr + vi
        h_scr[:, cre:cre + width] = hr
        h_scr[:, cim:cim + width] = hi
    yield

    y_even, y_odd = [], []
    for j in range(N_S5_BLOCKS):
        yy = _dot(v_scr[s, :, j * 1024:(j + 1) * 1024].astype(bf16), w2_ref[j]) + feeds[j]
        y_even.append(yy[:, :128])
        y_odd.append(yy[:, 128:])
    y = jnp.concatenate([jnp.concatenate(y_even, axis=-1), jnp.concatenate(y_odd, axis=-1)], axis=0)
    zs = jnp.concatenate([z_even, z_odd], axis=0)
    y = jax.nn.gelu(y + dskip_ref[...] * zs)
    y = y * jax.nn.sigmoid(_dot(y.astype(bf16), wglu_ref[...]))
    y_s5 = _rms(y)
    yield

    for t in range(steps):
        src = (t % 2) * half + (t // 2) * batch
        for k in range(n_slabs):
            if k * LANES < D_S5:
                v = y_s5[src:src + batch, k * LANES:(k + 1) * LANES]
            else:
                v = y_pool[t * batch:(t + 1) * batch, k * LANES - D_S5:(k + 1) * LANES - D_S5]
            perm_scr[s, k, pl.ds(t, batch, stride=pitch), :] = v
    mixed = jnp.concatenate(
        [jnp.concatenate([perm_scr[s, k, b * pitch:b * pitch + steps, :] for k in range(n_slabs)],
                         axis=-1) for b in range(batch)], axis=0).astype(bf16)
    x1_ref[:, frames, :] = (x + _dot(mixed, wout_ref[...])).reshape(batch, steps, D_MODEL)


SUBCHUNK_ORDER = {1: "", 2: "ABAAABBBAABBAB"}


def _mixer_kernel(*refs, batch, steps, n_sub, start_pos, n_cast):
    x_ref, h0re_ref, h0im_ref, hist_ref = refs[:4]
    wts = refs[4:4 + N_MIXER_W]
    cast_in = refs[4 + N_MIXER_W:4 + N_MIXER_W + n_cast]
    outs = refs[4 + N_MIXER_W + n_cast:]
    x1_ref, hre_ref, him_ref, histout_ref = outs[:4]
    cast_out = outs[4:4 + n_cast]
    v_scr, h_scr, xp_scr, zs_scr, perm_scr = outs[4 + n_cast:]
    c = pl.program_id(0)
    rows = batch * steps
    hist_rows = HIST_FRAMES * batch
    last = n_sub - 1

    for src, dst in zip(cast_in, cast_out):
        dst[...] = src[...].astype(bf16)

    def state_cols(j):
        lo = j * 2 * S5_BLOCK_STATES
        return slice(lo, lo + S5_BLOCK_STATES), slice(lo + S5_BLOCK_STATES, lo + 2 * S5_BLOCK_STATES)

    def block_cols(j):
        return slice(j * S5_BLOCK_STATES, (j + 1) * S5_BLOCK_STATES)

    @pl.when(c == 0)
    def _():
        for j in range(N_S5_BLOCKS):
            re_cols, im_cols = state_cols(j)
            h_scr[:, re_cols] = h0re_ref[:, block_cols(j)]
            h_scr[:, im_cols] = h0im_ref[:, block_cols(j)]
        xp_scr[last, rows:rows + hist_rows, :] = hist_ref[...]

    gens = [_mixer_phases(s, c, x_ref, wts, x1_ref, v_scr, h_scr, xp_scr, zs_scr, perm_scr,
                          batch=batch, steps=steps, n_sub=n_sub, start_pos=start_pos)
            for s in range(n_sub)]
    for tag in SUBCHUNK_ORDER[n_sub]:
        next(gens["AB".index(tag)], None)
    for gen in gens:
        for _ in gen:
            pass

    @pl.when(c == pl.num_programs(0) - 1)
    def _():
        for j in range(N_S5_BLOCKS):
            re_cols, im_cols = state_cols(j)
            hre_ref[:, block_cols(j)] = h_scr[:, re_cols]
            him_ref[:, block_cols(j)] = h_scr[:, im_cols]
        histout_ref[...] = xp_scr[last, rows:rows + hist_rows, :]


N_MIXER_W = 10


def _const_spec(shape):
    return pl.BlockSpec(shape, lambda c: (0,) * len(shape), pipeline_mode=pl.Buffered(1))


def _perm_pitch(steps):
    tiles = steps // SUBLANES
    return SUBLANES * (tiles + 1 if tiles % 2 == 0 else tiles + 2)


def _mixer(x, h0re, h0im, hist, start_pos, wts, to_bf16=()):
    batch, length, _ = x.shape
    assert MIXER_ROWS % batch == 0 and len(wts) == N_MIXER_W
    steps = MIXER_ROWS // batch
    n_sub = MIXER_SUBCHUNKS
    assert length % (steps * n_sub) == 0 and steps % SUBLANES == 0
    assert steps >= HIST_FRAMES and steps % 2 == 0 and STATE_VREG_ELEMS % batch == 0
    hist_rows = HIST_FRAMES * batch
    n_steps = length // (steps * n_sub)
    kern = functools.partial(_mixer_kernel, batch=batch, steps=steps, n_sub=n_sub, start_pos=start_pos,
                             n_cast=len(to_bf16))
    out = jax.ShapeDtypeStruct
    x_spec = pl.BlockSpec((batch, steps * n_sub, D_MODEL), lambda c: (0, c, 0))
    cast_specs = []
    for w, axis in to_bf16:
        size = w.shape[axis] // n_steps
        assert w.ndim == 2 and size * n_steps == w.shape[axis] and size % (LANES if axis else 2 * SUBLANES) == 0
        cast_specs.append(pl.BlockSpec((w.shape[0], size), lambda c: (0, c)) if axis else
                          pl.BlockSpec((size, w.shape[1]), lambda c: (c, 0)))
    return pl.pallas_call(
        kern,
        grid=(n_steps,),
        in_specs=[x_spec, _const_spec(h0re.shape), _const_spec(h0im.shape), _const_spec(hist.shape)]
        + [_const_spec(w.shape) for w in wts] + cast_specs,
        out_specs=(x_spec,
                   pl.BlockSpec(h0re.shape, lambda c: (0, 0)),
                   pl.BlockSpec(h0im.shape, lambda c: (0, 0)),
                   pl.BlockSpec(hist.shape, lambda c: (0, 0)), *cast_specs),
        out_shape=(out(x.shape, f32), out(h0re.shape, f32), out(h0im.shape, f32), out(hist.shape, f32),
                   *(out(w.shape, bf16) for w, _ in to_bf16)),
        scratch_shapes=[pltpu.VMEM((n_sub, MIXER_ROWS // 2, S5_COLS), f32),
                        pltpu.VMEM((batch, S5_COLS), f32),
                        pltpu.VMEM((n_sub, hist_rows + MIXER_ROWS, D_POOL), f32),
                        pltpu.VMEM((n_sub, 2, MIXER_ROWS // 2, D_S5), f32),
                        pltpu.VMEM((n_sub, D_MODEL // LANES, batch * _perm_pitch(steps), LANES), f32)],
        compiler_params=pltpu.CompilerParams(dimension_semantics=("arbitrary",),
                                             vmem_limit_bytes=VMEM_LIMIT_BYTES),
        name="mixer",
    )(x, h0re, h0im, hist, *wts, *(w for w, _ in to_bf16))


def _ffn_kernel(x_ref, p_ref, gmlp_ref, wup_ref, wdown_ref, gple_ref, wgate_ref, wproj_ref,
                gfinal_ref, o_ref):
    x = x_ref[...]
    xn = _rmsnorm(x, gmlp_ref[...]).astype(bf16)
    acc = x
    for f in range(D_FF // FFN_CHUNK):
        up = _dot(xn, wup_ref[:, f * FFN_CHUNK:(f + 1) * FFN_CHUNK])
        act = jnp.square(jnp.maximum(up, 0.0)).astype(bf16)
        acc = acc + _dot(act, wdown_ref[f * FFN_CHUNK:(f + 1) * FFN_CHUNK, :])
    for r in range(0, x.shape[0], FFN_TAIL_ROWS):
        rows = slice(r, r + FFN_TAIL_ROWS)
        a = acc[rows]
        gate = jax.nn.sigmoid(_dot(_rmsnorm(a, gple_ref[...]).astype(bf16), wgate_ref[...]))
        x3 = a + _dot(p_ref[rows, :].astype(bf16), wproj_ref[...]) * gate
        o_ref[rows, :] = _rmsnorm(x3, gfinal_ref[...])


def _ffn(x1, p, wts):
    n_rows = x1.shape[0]
    assert n_rows % FFN_ROWS == 0
    return pl.pallas_call(
        _ffn_kernel,
        grid=(n_rows // FFN_ROWS,),
        in_specs=[pl.BlockSpec((FFN_ROWS, D_MODEL), lambda i: (i, 0)),
                  pl.BlockSpec((FFN_ROWS, D_PLE), lambda i: (i, 0))] + [_const_spec(w.shape) for w in wts],
        out_specs=pl.BlockSpec((FFN_ROWS, D_MODEL), lambda i: (i, 0)),
        out_shape=jax.ShapeDtypeStruct((n_rows, D_MODEL), f32),
        compiler_params=pltpu.CompilerParams(dimension_semantics=("arbitrary",),
                                             vmem_limit_bytes=VMEM_LIMIT_BYTES),
        name="ffn",
    )(x1, p, *wts)


def _pack_hist(hist):
    b = hist.shape[0]
    h = jnp.pad(jnp.transpose(hist, (1, 0, 2)), ((1, 0), (0, 0), (0, 0)))
    return h.reshape(HIST_FRAMES * b, D_POOL)


def _unpack_hist(h, b):
    return jnp.transpose(h.reshape(HIST_FRAMES, b, D_POOL)[1:], (1, 0, 2))


def kernel(x_prompt, x_sample, state_s5_re, state_s5_im, state_pool, p_prompt, p_sample, g_mix_norm, w_in, lambda_re, lambda_im, log_dt, b_re, b_im, c_re, c_im, d_skip, w_glu, w_pool, pool_scale, g_s5_out, g_pool_out, w_out, g_mlp_norm, w_up, w_down, g_ple_norm, w_ple_gate, w_ple_proj, g_final):
    assert x_prompt.shape[2] == D_MODEL and w_in.shape[0] == 1
    bsz_p, bsz_s = x_prompt.shape[0], x_sample.shape[0]
    row = lambda v: v.reshape(1, -1)

    a2, w1, w2 = _s5_prep(lambda_re[0], lambda_im[0], log_dt[0], b_re[0], b_im[0], c_re[0], c_im[0])

    scale_rows = lambda w, g: (w * g[:, None]).astype(bf16)
    g_mixed = jnp.concatenate([g_s5_out[0], g_pool_out[0]])
    mixer_wts = (scale_rows(w_in[0], g_mix_norm[0]), row(a2[0]), row(a2[1]), w1, w2,
                 row(d_skip[0]), w_glu[0].astype(bf16), w_pool[0].astype(bf16), row(pool_scale[0]),
                 scale_rows(w_out[0], g_mixed))

    def mix(x, h0_re, h0_im, hist, start_pos, to_bf16=()):
        b = x.shape[0]
        flat = lambda h: h.reshape(b, N_S5_GROUPS * S5_STATE)
        x1, re, im, hist_new, *cast = _mixer(x, flat(h0_re), flat(h0_im), _pack_hist(hist), start_pos,
                                             mixer_wts, to_bf16)
        state = lambda h: h.reshape(1, b, N_S5_GROUPS, S5_STATE)
        return x1, state(re), state(im), _unpack_hist(hist_new, b)[None], cast

    def ffn(x1, p, ffn_wts):
        y = _ffn(x1.reshape(-1, D_MODEL), p.reshape(-1, D_PLE), ffn_wts)
        return y.reshape(x1.shape)

    zeros_state = jnp.zeros((bsz_p, N_S5_GROUPS, S5_STATE), f32)
    x1_p, re_p, im_p, pool_p, (wup, wdown, wgate) = mix(
        x_prompt, zeros_state, zeros_state, jnp.zeros((bsz_p, POOL_HIST, D_POOL), f32), 0,
        ((w_up[0], 1), (w_down[0], 0), (w_ple_gate[0], 0)))
    ffn_wts = (row(g_mlp_norm[0]), wup, wdown, row(g_ple_norm[0]), wgate, w_ple_proj[0].astype(bf16),
               row(g_final))
    y_p = ffn(x1_p, p_prompt[0], ffn_wts)
    x1_s, re_s, im_s, pool_s, _ = mix(x_sample, state_s5_re[0], state_s5_im[0], state_pool[0], PAST_LEN)
    y_s = ffn(x1_s, p_sample[0], ffn_wts)
    return (y_p, y_s, re_p, im_p, pool_p, re_s, im_s, pool_s)
```
